```python
import math
import jax
import jax.numpy as jnp
from jax import lax
import numpy as np

D_MODEL = 1024
BATCH = 16
SEQ = 2048
DEPTH = 1
DEC_BATCH = 32
DEC_SEQ = 4
PAST_LEN = 16384
PAGE_SIZE = 128

SB_HEADS = 8
SB_HEAD_DIM = 64
SB_WIDTH = SB_HEADS * SB_HEAD_DIM
SB_Q_BLOCK = 128
SB_BIAS_INIT = -6.0
HG_HEADS = 8
HG_DK = 64
HG_DV = 64
HG_WIDTH_K = HG_HEADS * HG_DK
HG_WIDTH_V = HG_HEADS * HG_DV
HG_CHUNK = 32
N_GROUPS = 4
EXPERTS_PER_GROUP = 8
N_EXPERTS = N_GROUPS * EXPERTS_PER_GROUP
TOP_K = 2
D_FF_EXPERT = 512
MOE_BLOCK = 64
RMS_EPS = 1e-6
D_IN = 3 * SB_WIDTH + 2 * HG_WIDTH_K + 2 * HG_WIDTH_V + 2 * D_MODEL

kernel_name = "hybrid_stickbreak_hgrn2_hmoe_step"


def rms_norm(x, w):
    xf = x.astype(jnp.float32)
    y = xf * lax.rsqrt(jnp.mean(xf * xf, axis=-1, keepdims=True) + RMS_EPS)
    return (y * w.astype(jnp.float32)).astype(x.dtype)


def mixer_inputs(x, norm1_w, w_in, q_norm_w, k_norm_w, lb):
    B, T, _ = x.shape
    h = rms_norm(x, norm1_w) @ w_in
    sizes = (SB_WIDTH, SB_WIDTH, SB_WIDTH, HG_WIDTH_K, HG_WIDTH_K, HG_WIDTH_V, HG_WIDTH_V, D_MODEL)
    pts = []
    acc = 0
    for s in sizes:
        acc += s
        pts.append(acc)
    q_sb, k_sb, v_sb, q_hg, f_hg, i_hg, g_hg, gate_a, gate_b = jnp.split(h, pts, axis=-1)
    q_sb = rms_norm(q_sb.reshape(B, T, SB_HEADS, SB_HEAD_DIM), q_norm_w)
    k_sb = rms_norm(k_sb.reshape(B, T, SB_HEADS, SB_HEAD_DIM), k_norm_w)
    v_sb = v_sb.reshape(B, T, SB_HEADS, SB_HEAD_DIM)
    f = lb + (1.0 - lb) * jax.nn.sigmoid(f_hg.astype(jnp.float32))
    log_f = jnp.log(f).reshape(B, T, HG_HEADS, HG_DK)
    k_hg = (1.0 - f).reshape(B, T, HG_HEADS, HG_DK)
    q_hg = q_hg.astype(jnp.float32).reshape(B, T, HG_HEADS, HG_DK)
    i_hg = i_hg.astype(jnp.float32).reshape(B, T, HG_HEADS, HG_DV)
    return q_sb, k_sb, v_sb, q_hg, log_f, k_hg, i_hg, g_hg, gate_a, gate_b


def stick_breaking_weights(z, mask):
    log_keep = jnp.where(mask, jax.nn.log_sigmoid(-z), 0.0)
    remain = lax.cumsum(log_keep, axis=z.ndim - 1, reverse=True)
    return jnp.exp(jnp.where(mask, z + remain, -jnp.inf))


def sb_prompt(q, k, v, bias):
    B, T, H, Dh = q.shape
    scale = Dh ** -0.5
    b = bias.astype(jnp.float32)[None, :, None, None]
    outs = []
    for blk in range(T // SB_Q_BLOCK):
        lo = blk * SB_Q_BLOCK
        hi = lo + SB_Q_BLOCK
        z = jnp.einsum('bqhd,bkhd->bhqk', q[:, lo:hi], k[:, :hi]).astype(jnp.float32) * scale + b
        mask = jnp.arange(hi)[None, :] < jnp.arange(lo, hi)[:, None]
        a = stick_breaking_weights(z, mask).astype(v.dtype)
        outs.append(jnp.einsum('bhqk,bkhd->bqhd', a, v[:, :hi]))
    return jnp.concatenate(outs, axis=1)


def sb_decode(q, k_new, v_new, k_past, v_past, bias):
    Tq = q.shape[1]
    P = k_past.shape[1]
    scale = q.shape[-1] ** -0.5
    z_past = jnp.einsum('bqhd,bkhd->bhqk', q, k_past).astype(jnp.float32)
    z_new = jnp.einsum('bqhd,bkhd->bhqk', q, k_new).astype(jnp.float32)
    z = jnp.concatenate([z_past, z_new], axis=-1) * scale + bias.astype(jnp.float32)[None, :, None, None]
    mask = jnp.concatenate([jnp.ones((Tq, P), dtype=bool),
                            jnp.arange(Tq)[None, :] < jnp.arange(Tq)[:, None]], axis=-1)
    a = stick_breaking_weights(z, mask).astype(v_new.dtype)
    return (jnp.einsum('bhqk,bkhd->bqhd', a[..., :P], v_past)
            + jnp.einsum('bhqk,bkhd->bqhd', a[..., P:], v_new))


def hgrn2_chunked(q, log_f, k, v, s0):
    B, T, H, Dk = q.shape
    Dv = v.shape[-1]
    c = HG_CHUNK if T % HG_CHUNK == 0 else T
    n = T // c

    def to_chunks(a):
        return a.reshape(B, n, c, H, a.shape[-1]).transpose(1, 0, 3, 2, 4)

    causal = jnp.tril(jnp.ones((c, c), dtype=bool))

    def step(s, inp):
        qc, gc, kc, vc = inp
        g_cum = jnp.cumsum(gc, axis=2)
        g_last = g_cum[:, :, -1:, :]
        q_dec = qc * jnp.exp(g_cum)
        k_dec = kc * jnp.exp(-g_cum)
        scores = jnp.where(causal, jnp.einsum('bhtk,bhsk->bhts', q_dec, k_dec), 0.0)
        o = jnp.einsum('bhtk,bhkv->bhtv', q_dec, s) + jnp.einsum('bhts,bhsv->bhtv', scores, vc)
        s_new = (jnp.exp(g_last[:, :, 0, :])[..., None] * s
                 + jnp.einsum('bhsk,bhsv->bhkv', kc * jnp.exp(g_last - g_cum), vc))
        return s_new, o

    s_fin, o = lax.scan(step, s0, (to_chunks(q), to_chunks(log_f), to_chunks(k), to_chunks(v)))
    o = o.transpose(1, 0, 3, 2, 4).reshape(B, T, H, Dv)
    return o, s_fin


def mixer_output(x, o_sb, o_hg, g_hg, gate_a, gate_b, hg_norm_w, w_branch_a, w_branch_b, w_out):
    B, T, _ = x.shape
    y_b = (rms_norm(o_hg, hg_norm_w).reshape(B, T, HG_WIDTH_V)
           * jax.nn.silu(g_hg.astype(jnp.float32))).astype(x.dtype)
    ya = o_sb.reshape(B, T, SB_WIDTH) @ w_branch_a
    yb = y_b @ w_branch_b
    merged = jax.nn.sigmoid(gate_a) * ya + jax.nn.sigmoid(gate_b) * yb
    return x + merged @ w_out


def hier_moe(x2d, w_gr, b_gr, w_er, b_er, w_g, w_u, w_d):
    n_tok, d = x2d.shape
    g_logits = (x2d @ w_gr).astype(jnp.float32) + b_gr.astype(jnp.float32)
    g_prob = jax.nn.softmax(g_logits, axis=-1)
    p_group, g_idx = lax.top_k(g_prob, 1)
    e_logits = (x2d @ w_er).astype(jnp.float32) + b_er.astype(jnp.float32)
    e_logits = e_logits.reshape(n_tok, N_GROUPS, EXPERTS_PER_GROUP)[jnp.arange(n_tok), g_idx[:, 0]]
    e_prob = jax.nn.softmax(e_logits, axis=-1)
    w_top, e_loc = lax.top_k(e_prob, TOP_K)
    w_top = p_group * w_top / jnp.sum(w_top, axis=-1, keepdims=True)
    e_glob = g_idx * EXPERTS_PER_GROUP + e_loc

    n_assign = n_tok * TOP_K
    e_flat = e_glob.reshape(-1)
    w_flat = w_top.reshape(-1)
    tok_flat = jnp.repeat(jnp.arange(n_tok, dtype=jnp.int32), TOP_K)
    order = jnp.argsort(e_flat)
    e_sorted = e_flat[order]
    counts = jnp.bincount(e_flat, length=N_EXPERTS)
    starts = jnp.cumsum(counts) - counts
    padded = ((counts + MOE_BLOCK - 1) // MOE_BLOCK) * MOE_BLOCK
    pad_ends = jnp.cumsum(padded)
    pad_starts = pad_ends - padded
    dest = pad_starts[e_sorted] + (jnp.arange(n_assign) - starts[e_sorted])
    n_blocks = -(-(n_assign + N_EXPERTS * (MOE_BLOCK - 1)) // MOE_BLOCK)
    n_rows = n_blocks * MOE_BLOCK
    row_tok = jnp.zeros((n_rows,), jnp.int32).at[dest].set(tok_flat[order])
    row_w = jnp.zeros((n_rows,), jnp.float32).at[dest].set(w_flat[order])
    block_exp = jnp.clip(jnp.searchsorted(pad_ends, jnp.arange(n_blocks) * MOE_BLOCK, side='right'),
                         0, N_EXPERTS - 1)

    def run_block(args):
        toks, e = args
        xb = x2d[toks]
        hb = jax.nn.silu(xb @ w_g[e]) * (xb @ w_u[e])
        return hb @ w_d[e]

    out = lax.map(run_block, (row_tok.reshape(n_blocks, MOE_BLOCK), block_exp))
    contrib = (out.reshape(n_rows, d).astype(jnp.float32) * row_w[:, None]).astype(x2d.dtype)
    return jnp.zeros((n_tok, d), x2d.dtype).at[row_tok].add(contrib)


def channel_mixer(x, norm2_w, w_gr, b_gr, w_er, b_er, w_g, w_u, w_d):
    B, T, d = x.shape
    h = rms_norm(x, norm2_w).reshape(B * T, d)
    return x + hier_moe(h, w_gr, b_gr, w_er, b_er, w_g, w_u, w_d).reshape(B, T, d)


def setup_inputs(seed: int = 0) -> dict:
    key = jax.random.key(seed)
    ks = jax.random.split(key, 24)
    f32 = jnp.float32
    n_pages = PAST_LEN // PAGE_SIZE
    n_used = DEC_BATCH * n_pages
    n_pool = n_used + n_used // 4

    def nrm(k, shape, scale):
        return jax.random.normal(k, shape, f32) * scale

    return {
        "x_prompt": nrm(ks[0], (BATCH, SEQ, D_MODEL), 1.0),
        "x_sample": nrm(ks[1], (DEC_BATCH, DEC_SEQ, D_MODEL), 1.0),
        "cache_k": nrm(ks[2], (DEPTH, n_pool, PAGE_SIZE, SB_HEADS, SB_HEAD_DIM), 1.0),
        "cache_v": nrm(ks[3], (DEPTH, n_pool, PAGE_SIZE, SB_HEADS, SB_HEAD_DIM), 1.0),
        "page_table": jax.random.permutation(ks[4], n_pool)[:n_used].reshape(DEC_BATCH, n_pages).astype(jnp.int32),
        "state_hgrn": nrm(ks[5], (DEPTH, DEC_BATCH, HG_HEADS, HG_DK, HG_DV), 0.5),
        "norm1_w": 1.0 + nrm(ks[6], (DEPTH, D_MODEL), 0.01),
        "w_in": nrm(ks[7], (DEPTH, D_MODEL, D_IN), D_MODEL ** -0.5),
        "q_norm_w": 1.0 + nrm(ks[8], (DEPTH, SB_HEAD_DIM), 0.01),
        "k_norm_w": 1.0 + nrm(ks[9], (DEPTH, SB_HEAD_DIM), 0.01),
        "sb_bias": SB_BIAS_INIT + nrm(ks[23], (DEPTH, SB_HEADS), 0.1),
        "lb_logits": nrm(ks[10], (DEPTH + 1, HG_WIDTH_K), 0.1),
        "hg_norm_w": 1.0 + nrm(ks[11], (DEPTH, HG_DV), 0.01),
        "w_branch_a": nrm(ks[12], (DEPTH, SB_WIDTH, D_MODEL), SB_WIDTH ** -0.5),
        "w_branch_b": nrm(ks[13], (DEPTH, HG_WIDTH_V, D_MODEL), HG_WIDTH_V ** -0.5),
        "w_out": nrm(ks[14], (DEPTH, D_MODEL, D_MODEL), D_MODEL ** -0.5),
        "norm2_w": 1.0 + nrm(ks[15], (DEPTH, D_MODEL), 0.01),
        "w_group_router": nrm(ks[16], (DEPTH, D_MODEL, N_GROUPS), D_MODEL ** -0.5),
        "b_group_router": nrm(ks[17], (DEPTH, N_GROUPS), 0.01),
        "w_expert_router": nrm(ks[18], (DEPTH, D_MODEL, N_EXPERTS), D_MODEL ** -0.5),
        "b_expert_router": nrm(ks[19], (DEPTH, N_EXPERTS), 0.01),
        "w_gate_exp": nrm(ks[20], (DEPTH, N_EXPERTS, D_MODEL, D_FF_EXPERT), D_MODEL ** -0.5),
        "w_up_exp": nrm(ks[21], (DEPTH, N_EXPERTS, D_MODEL, D_FF_EXPERT), D_MODEL ** -0.5),
        "w_down_exp": nrm(ks[22], (DEPTH, N_EXPERTS, D_FF_EXPERT, D_MODEL), D_FF_EXPERT ** -0.5),
    }


def reference(x_prompt, x_sample, cache_k, cache_v, page_table, state_hgrn, norm1_w, w_in, q_norm_w,
              k_norm_w, sb_bias, lb_logits, hg_norm_w, w_branch_a, w_branch_b, w_out, norm2_w, w_group_router,
              b_group_router, w_expert_router, b_expert_router, w_gate_exp, w_up_exp, w_down_exp):
    xp = x_prompt
    xs = x_sample
    B = xp.shape[0]
    Bd = xs.shape[0]
    n_pages = page_table.shape[1]
    lbs = jnp.cumsum(jax.nn.softmax(lb_logits.astype(jnp.float32), axis=0), axis=0)
    kp_l, vp_l, sp_l, ks_l, vs_l, ss_l = [], [], [], [], [], []
    for l in range(DEPTH):
        lb = lbs[l]
        q, k, v, hq, hf, hk, hi, hg, ga, gb = mixer_inputs(xp, norm1_w[l], w_in[l], q_norm_w[l], k_norm_w[l], lb)
        o_sb = sb_prompt(q, k, v, sb_bias[l])
        o_hg, s_p = hgrn2_chunked(hq, hf, hk, hi, jnp.zeros((B, HG_HEADS, HG_DK, HG_DV), jnp.float32))
        xp = mixer_output(xp, o_sb, o_hg, hg, ga, gb, hg_norm_w[l], w_branch_a[l], w_branch_b[l], w_out[l])
        xp = channel_mixer(xp, norm2_w[l], w_group_router[l], b_group_router[l], w_expert_router[l],
                           b_expert_router[l], w_gate_exp[l], w_up_exp[l], w_down_exp[l])
        kp_l.append(k)
        vp_l.append(v)
        sp_l.append(s_p)
        q, k, v, hq, hf, hk, hi, hg, ga, gb = mixer_inputs(xs, norm1_w[l], w_in[l], q_norm_w[l], k_norm_w[l], lb)
        k_past = cache_k[l][page_table].reshape(Bd, n_pages * PAGE_SIZE, SB_HEADS, SB_HEAD_DIM)
        v_past = cache_v[l][page_table].reshape(Bd, n_pages * PAGE_SIZE, SB_HEADS, SB_HEAD_DIM)
        o_sb = sb_decode(q, k, v, k_past, v_past, sb_bias[l])
        o_hg, s_s = hgrn2_chunked(hq, hf, hk, hi, state_hgrn[l].astype(jnp.float32))
        xs = mixer_output(xs, o_sb, o_hg, hg, ga, gb, hg_norm_w[l], w_branch_a[l], w_branch_b[l], w_out[l])
        xs = channel_mixer(xs, norm2_w[l], w_group_router[l], b_group_router[l], w_expert_router[l],
                           b_expert_router[l], w_gate_exp[l], w_up_exp[l], w_down_exp[l])
        ks_l.append(k)
        vs_l.append(v)
        ss_l.append(s_s)
    new_k_prompt = jnp.stack(kp_l, axis=0)
    new_v_prompt = jnp.stack(vp_l, axis=0)
    new_hgrn_prompt = jnp.stack(sp_l, axis=0)
    new_k_sample = jnp.stack(ks_l, axis=0)
    new_v_sample = jnp.stack(vs_l, axis=0)
    new_hgrn_sample = jnp.stack(ss_l, axis=0)
    return (xp, xs, new_k_prompt, new_v_prompt, new_hgrn_prompt, new_k_sample, new_v_sample, new_hgrn_sample)
```

```python
import functools

import jax
import jax.numpy as jnp
from jax import lax
from jax.experimental import pallas as pl
from jax.experimental.pallas import tpu as pltpu

F32 = jnp.float32
BF16 = jnp.bfloat16
I32 = jnp.int32

D_MODEL = 1024
N_HEADS = 8
HEAD_DIM = 64
WIDTH = N_HEADS * HEAD_DIM
LANES = 128
N_PAIRS = WIDTH // LANES
N_GROUPS = 4
EXPERTS_PER_GROUP = 8
N_EXPERTS = N_GROUPS * EXPERTS_PER_GROUP
D_FF = 512
PAGE = 128
HG_CHUNK = 32
RMS_EPS = 1e-6
NEG = -1e30

VMEM_LIMIT = 56 * 1024 * 1024
ROW_TILE = 256
SB_TILE = 256
HG_ROWS = 256
MOE_TILE = 256
PAGES_PER_STEP = 16
DEC_CHUNK = 16
ROUTER_ROWS = 48


def _cparams(sem):
    return pltpu.CompilerParams(dimension_semantics=sem, vmem_limit_bytes=VMEM_LIMIT)


def _dot(a, b):
    return jnp.dot(a, b, preferred_element_type=F32)


def _dot_nt(a, b):
    return lax.dot_general(a, b, (((1,), (1,)), ((), ())), preferred_element_type=F32)


def _split_bf16(x):
    hi = x.astype(BF16)
    lo = (x - hi.astype(F32)).astype(BF16)
    return hi, lo


def _dot_split(x, m):
    hi, lo = _split_bf16(x)
    return _dot(hi, m) + _dot(lo, m)


def _head_mean_sq(h, bd):
    return _dot_split(h * h, bd) * (1.0 / HEAD_DIM)


def _rms_rows(x, w):
    ms = jnp.mean(x * x, axis=-1, keepdims=True)
    return x * lax.rsqrt(ms + RMS_EPS) * w


def _neg_softplus(z):
    return -(jnp.maximum(z, 0.0) + jnp.log1p(jnp.exp(-jnp.abs(z))))


_SEGS = {"q": (0, 512), "k": (512, 1024), "v": (1024, 1536), "hq": (1536, 2048), "hf": (2048, 2560),
         "hi": (2560, 3072), "hg": (3072, 3584), "ga": (3584, 4608), "gb": (4608, 5632)}
D_IN = 5632


def _inproj_kernel(x_ref, n1_ref, w_ref, bd_ref, qw_ref, kw_ref,
                   q_ref, k_ref, kb_ref, v_ref, vb_ref, hq_ref, hf_ref, hi_ref, hg_ref, ga_ref, gb_ref):
    xn = _rms_rows(x_ref[...], n1_ref[...]).astype(BF16)
    bd = bd_ref[...]

    def seg(name):
        a, b = _SEGS[name]
        return _dot(xn, w_ref[:, a:b])

    hq = seg("q")
    qn = hq * lax.rsqrt(_head_mean_sq(hq, bd) + RMS_EPS) * qw_ref[...]
    q_ref[...] = (qn * HEAD_DIM ** -0.5).astype(BF16)
    hk = seg("k")
    kn = hk * lax.rsqrt(_head_mean_sq(hk, bd) + RMS_EPS) * kw_ref[...]
    k_ref[...] = kn
    kb_ref[...] = kn.astype(BF16)
    hv = seg("v")
    v_ref[...] = hv
    vb_ref[...] = hv.astype(BF16)
    hq_ref[...] = seg("hq")
    hf_ref[...] = seg("hf")
    hi_ref[...] = seg("hi")
    hg_ref[...] = seg("hg")
    ga_ref[...] = seg("ga")
    gb_ref[...] = seg("gb")


def _in_proj(x2d, n1, w_in, bd, qw, kw):
    n = x2d.shape[0]
    tm = min(ROW_TILE, n)
    row = lambda w: pl.BlockSpec((tm, w), lambda i: (i, 0))
    full = lambda a: pl.BlockSpec(a.shape, lambda i: (0,) * a.ndim)
    outs = [(WIDTH, BF16), (WIDTH, F32), (WIDTH, BF16), (WIDTH, F32), (WIDTH, BF16),
            (WIDTH, F32), (WIDTH, F32), (WIDTH, F32), (WIDTH, F32), (D_MODEL, F32), (D_MODEL, F32)]
    return pl.pallas_call(
        _inproj_kernel,
        grid=(n // tm,),
        in_specs=[row(D_MODEL), full(n1), full(w_in), full(bd), full(qw), full(kw)],
        out_specs=[row(w) for w, _ in outs],
        out_shape=[jax.ShapeDtypeStruct((n, w), dt) for w, dt in outs],
        compiler_params=_cparams(("parallel",)),
        name="in_proj",
    )(x2d, n1, w_in, bd, qw, kw)


def _sb_prompt_kernel(bias_ref, q_ref, k_ref, v_ref, tri_ref, o_ref, *, tile):
    p = pl.program_id(1)
    i = pl.program_id(2)
    q = q_ref[0]
    lane = lax.broadcasted_iota(I32, q.shape, 1)
    first = lane < HEAD_DIM
    zero = jnp.zeros_like(q)
    qs = (jnp.where(first, q, zero), jnp.where(first, zero, q))
    bias = (bias_ref[2 * p], bias_ref[2 * p + 1])
    tri = tri_ref[...]
    row = lax.broadcasted_iota(I32, (tile, tile), 0)
    col = lax.broadcasted_iota(I32, (tile, tile), 1)
    causal = col < row

    def block(kb, vb, carry, accs, mask):
        new_c, new_a = [], []
        for h in range(2):
            z = _dot_nt(qs[h], kb) + bias[h]
            lk = _neg_softplus(z)
            if mask is not None:
                lk = jnp.where(mask, lk, 0.0)
            rem = _dot_split(lk, tri) + carry[h]
            a = jnp.exp(z + rem)
            if mask is not None:
                a = jnp.where(mask, a, 0.0)
            new_a.append(accs[h] + _dot(a.astype(BF16), vb))
            new_c.append(rem[:, 0:1])
        return tuple(new_c), tuple(new_a)

    start = pl.multiple_of(i * tile, tile)
    zc = jnp.zeros((tile, 1), F32)
    za = jnp.zeros((tile, LANES), F32)
    carry, accs = block(k_ref[0, pl.ds(start, tile), :], v_ref[0, pl.ds(start, tile), :],
                        (zc, zc), (za, za), causal)

    def body(n, state):
        c, a = state
        s = pl.multiple_of((i - 1 - n) * tile, tile)
        return block(k_ref[0, pl.ds(s, tile), :], v_ref[0, pl.ds(s, tile), :], c, a, None)

    carry, accs = lax.fori_loop(0, i, body, (carry, accs))
    o_ref[0] = jnp.where(first, accs[0], accs[1]).astype(BF16)


def _sb_prompt(q, k, v, bias, tri):
    b, t, _ = q.shape
    tile = min(SB_TILE, t)
    kern = functools.partial(_sb_prompt_kernel, tile=tile)
    return pl.pallas_call(
        kern,
        grid_spec=pltpu.PrefetchScalarGridSpec(
            num_scalar_prefetch=0,
            grid=(b, N_PAIRS, t // tile),
            in_specs=[pl.BlockSpec(memory_space=pltpu.SMEM),
                      pl.BlockSpec((1, tile, LANES), lambda bi, p, i: (bi, i, p)),
                      pl.BlockSpec((1, t, LANES), lambda bi, p, i: (bi, 0, p)),
                      pl.BlockSpec((1, t, LANES), lambda bi, p, i: (bi, 0, p)),
                      pl.BlockSpec((tile, tile), lambda bi, p, i: (0, 0))],
            out_specs=pl.BlockSpec((1, tile, LANES), lambda bi, p, i: (bi, i, p)),
        ),
        out_shape=jax.ShapeDtypeStruct((b, t, WIDTH), BF16),
        compiler_params=_cparams(("parallel", "parallel", "arbitrary")),
        name="sb_prompt",
    )(bias, q, k, v, tri)


def _sb_decode_kernel(pt_ref, bias_ref, q_ref, kn_ref, vn_ref, ck_hbm, cv_hbm, trip_ref, trin_ref, o_ref,
                      kbuf, vbuf, sem, qbd_s, bias_s, acc_s, carry_s, *, n_pages, n_q):
    b = pl.program_id(0)
    s = pl.program_id(1)
    nb = pl.num_programs(0)
    ns = pl.num_programs(1)
    pp = kbuf.shape[1]
    g = b * ns + s
    slot = lax.rem(g, 2)
    rows = n_q * N_HEADS

    def page_copies(bb, ss, sl, i):
        page = pt_ref[bb, n_pages - (ss + 1) * pp + i]
        return (pltpu.make_async_copy(ck_hbm.at[page], kbuf.at[sl, i], sem.at[0, sl]),
                pltpu.make_async_copy(cv_hbm.at[page], vbuf.at[sl, i], sem.at[1, sl]))

    def issue(bb, ss, sl):
        for i in range(pp):
            ck, cv = page_copies(bb, ss, sl, i)
            ck.start()
            cv.start()

    @pl.when(g == 0)
    def _():
        issue(0, 0, 0)

    @pl.when(g + 1 < nb * ns)
    def _():
        last = s + 1 == ns
        issue(jnp.where(last, b + 1, b), jnp.where(last, 0, s + 1), 1 - slot)

    head_of_lane = lax.broadcasted_iota(I32, (N_HEADS, WIDTH), 1) >> 6
    head_of_row = lax.broadcasted_iota(I32, (N_HEADS, WIDTH), 0)
    head_mask = head_of_lane == head_of_row

    @pl.when(s == 0)
    def _():
        qf = q_ref[0].astype(F32)
        for qi in range(n_q):
            blk = jnp.where(head_mask, jnp.broadcast_to(qf[qi:qi + 1, :], (N_HEADS, WIDTH)), 0.0)
            qbd_s[qi * N_HEADS:(qi + 1) * N_HEADS, :] = blk.astype(BF16)
        r = lax.broadcasted_iota(I32, (rows, LANES), 0) & (N_HEADS - 1)
        bv = jnp.zeros((rows, LANES), F32)
        for h in range(N_HEADS):
            bv = jnp.where(r == h, bias_ref[h], bv)
        bias_s[...] = bv
        z = _dot_nt(qbd_s[...], kn_ref[0]) + bv
        key = lax.broadcasted_iota(I32, (rows, LANES), 1)
        qidx = lax.broadcasted_iota(I32, (rows, LANES), 0) >> 3
        mask = key < qidx
        lk = jnp.where(mask, _neg_softplus(z), 0.0)
        rem = _dot_split(lk, trin_ref[...])
        a = jnp.where(mask, jnp.exp(z + rem), 0.0)
        acc_s[...] = _dot(a.astype(BF16), vn_ref[0])
        carry_s[...] = jnp.broadcast_to(rem[:, 0:1], (rows, LANES))

    for i in range(pp):
        ck, cv = page_copies(b, s, slot, i)
        ck.wait()
        cv.wait()

    qbd = qbd_s[...]
    trip = trip_ref[...]
    bias2 = jnp.concatenate([bias_s[...], bias_s[...]], axis=1)

    def body(n, _):
        j = pl.multiple_of((pp // 2 - 1 - n) * 2, 2)
        kb = kbuf[slot, pl.ds(j, 2)].reshape(2 * PAGE, WIDTH).astype(BF16)
        vb = vbuf[slot, pl.ds(j, 2)].reshape(2 * PAGE, WIDTH).astype(BF16)
        z = _dot_nt(qbd, kb) + bias2
        lk = _neg_softplus(z)
        carry = carry_s[...]
        rem = _dot_split(lk, trip) + jnp.concatenate([carry, carry], axis=1)
        a = jnp.exp(z + rem)
        acc_s[...] += _dot(a.astype(BF16), vb)
        carry_s[...] = jnp.broadcast_to(rem[:, 0:1], (rows, LANES))
        return 0

    lax.fori_loop(0, pp // 2, body, 0)

    @pl.when(s == ns - 1)
    def _():
        acc = acc_s[...]
        for qi in range(n_q):
            blk = jnp.where(head_mask, acc[qi * N_HEADS:(qi + 1) * N_HEADS, :], 0.0)
            o_ref[0, qi:qi + 1, :] = jnp.sum(blk, axis=0, keepdims=True)


def _sb_decode(q, k_new, v_new, cache_k, cache_v, page_table, bias, trip, trin):
    bd, n_q, _ = q.shape
    n_pages = page_table.shape[1]
    pp = min(PAGES_PER_STEP, n_pages)
    rows = n_q * N_HEADS
    kern = functools.partial(_sb_decode_kernel, n_pages=n_pages, n_q=n_q)
    return pl.pallas_call(
        kern,
        grid_spec=pltpu.PrefetchScalarGridSpec(
            num_scalar_prefetch=1,
            grid=(bd, n_pages // pp),
            in_specs=[pl.BlockSpec(memory_space=pltpu.SMEM),
                      pl.BlockSpec((1, n_q, WIDTH), lambda b, s, pt: (b, 0, 0)),
                      pl.BlockSpec((1, LANES, WIDTH), lambda b, s, pt: (b, 0, 0)),
                      pl.BlockSpec((1, LANES, WIDTH), lambda b, s, pt: (b, 0, 0)),
                      pl.BlockSpec(memory_space=pl.ANY),
                      pl.BlockSpec(memory_space=pl.ANY),
                      pl.BlockSpec((2 * PAGE, 2 * PAGE), lambda b, s, pt: (0, 0)),
                      pl.BlockSpec((LANES, LANES), lambda b, s, pt: (0, 0))],
            out_specs=pl.BlockSpec((1, n_q, WIDTH), lambda b, s, pt: (b, 0, 0)),
            scratch_shapes=[pltpu.VMEM((2, pp, PAGE, WIDTH), F32),
                            pltpu.VMEM((2, pp, PAGE, WIDTH), F32),
                            pltpu.SemaphoreType.DMA((2, 2)),
                            pltpu.VMEM((rows, WIDTH), BF16),
                            pltpu.VMEM((rows, LANES), F32),
                            pltpu.VMEM((rows, WIDTH), F32),
                            pltpu.VMEM((rows, LANES), F32)],
        ),
        out_shape=jax.ShapeDtypeStruct((bd, n_q, WIDTH), F32),
        compiler_params=_cparams(("arbitrary", "arbitrary")),
        name="sb_decode",
    )(page_table, bias, q, k_new, v_new, cache_k, cache_v, trip, trin)


def _hgrn_rows(hq, hf, hi, lb, ltri, llast):
    f = lb + (1.0 - lb) * jax.nn.sigmoid(hf)
    g = jnp.log(f)
    kk = 1.0 - f
    g_hi, g_lo = _split_bf16(g)
    gcum = _dot(ltri, g_hi) + _dot(ltri, g_lo)
    glast = _dot(llast, g_hi) + _dot(llast, g_lo)
    q_dec = hq * jnp.exp(gcum)
    k_dec = kk * jnp.exp(-gcum)
    k_rem = kk * jnp.exp(glast - gcum)
    return q_dec, k_dec, k_rem, glast


def _hgrn_pair(p, q_dec, k_dec, k_rem, glast, hi, cmask, chunk, get_state, put_state):
    n_rows = q_dec.shape[0]
    sl = slice(p * LANES, (p + 1) * LANES)
    lane = lax.broadcasted_iota(I32, (n_rows, LANES), 1)
    first = lane < HEAD_DIM
    q = q_dec[:, sl]
    kb = k_dec[:, sl].astype(BF16)
    v = hi[:, sl]
    vb = v.astype(BF16)
    kr = k_rem[:, sl]
    gl = glast[:, sl]
    q0 = jnp.where(first, q, 0.0).astype(BF16)
    q1 = jnp.where(first, 0.0, q).astype(BF16)
    p0 = jnp.where(cmask, _dot_nt(q0, kb), 0.0).astype(BF16)
    p1 = jnp.where(cmask, _dot_nt(q1, kb), 0.0).astype(BF16)
    o_intra = jnp.where(first, _dot(p0, vb), _dot(p1, vb))
    qb = q.astype(BF16)
    vt = v.T.astype(BF16)
    row_chunk = lax.broadcasted_iota(I32, (n_rows, LANES), 0) >> (chunk.bit_length() - 1)
    sr = lax.broadcasted_iota(I32, (LANES, LANES), 0) >> 6
    sc = lax.broadcasted_iota(I32, (LANES, LANES), 1) >> 6
    same_head = sr == sc
    parts = []
    for c in range(n_rows // chunk):
        rs = slice(c * chunk, (c + 1) * chunk)
        st = get_state(c)
        parts.append(_dot_nt(qb[rs], st.astype(BF16)))
        krc = jnp.where(row_chunk == c, kr, 0.0).astype(BF16)
        upd = _dot(vt, krc)
        decay = jnp.exp(gl[c * chunk:c * chunk + 1, :])
        put_state(c, st * decay + jnp.where(same_head, upd, 0.0))
    return o_intra + jnp.concatenate(parts, axis=0)


def _hgrn_finish(o, gate, bd, hgw):
    y = o * lax.rsqrt(_head_mean_sq(o, bd) + RMS_EPS) * hgw
    return (y * (gate * jax.nn.sigmoid(gate))).astype(BF16)


def _hgrn_prompt_kernel(hq_ref, hf_ref, hi_ref, hg_ref, lb_ref, hgw_ref, ltri_ref, llast_ref, cm_ref, bd_ref,
                        y_ref, st_ref):
    @pl.when(pl.program_id(1) == 0)
    def _():
        st_ref[...] = jnp.zeros_like(st_ref)

    hi = hi_ref[0]
    q_dec, k_dec, k_rem, glast = _hgrn_rows(hq_ref[0], hf_ref[0], hi, lb_ref[...], ltri_ref[...], llast_ref[...])
    cmask = cm_ref[...] > 0
    outs = []
    for p in range(N_PAIRS):
        def get_state(c, p=p):
            return st_ref[0, p]

        def put_state(c, val, p=p):
            st_ref[0, p] = val

        outs.append(_hgrn_pair(p, q_dec, k_dec, k_rem, glast, hi, cmask, HG_CHUNK, get_state, put_state))
    o = jnp.concatenate(outs, axis=1)
    y_ref[0] = _hgrn_finish(o, hg_ref[0], bd_ref[...], hgw_ref[...])


def _hgrn_prompt(hq, hf, hi, hg, lb, hgw, masks, bd):
    b, t, _ = hq.shape
    rows = min(HG_ROWS, t)
    ltri, llast, cm = masks
    blk = pl.BlockSpec((1, rows, WIDTH), lambda bi, j: (bi, j, 0))
    full = lambda a: pl.BlockSpec(a.shape, lambda bi, j: (0,) * a.ndim)
    return pl.pallas_call(
        _hgrn_prompt_kernel,
        grid=(b, t // rows),
        in_specs=[blk, blk, blk, blk, full(lb), full(hgw), full(ltri), full(llast), full(cm), full(bd)],
        out_specs=[blk, pl.BlockSpec((1, N_PAIRS, LANES, LANES), lambda bi, j: (bi, 0, 0, 0))],
        out_shape=[jax.ShapeDtypeStruct((b, t, WIDTH), BF16),
                   jax.ShapeDtypeStruct((b, N_PAIRS, LANES, LANES), F32)],
        compiler_params=_cparams(("parallel", "arbitrary")),
        name="hgrn_prompt",
    )(hq, hf, hi, hg, lb, hgw, ltri, llast, cm, bd)


def _hgrn_decode_kernel(hq_ref, hf_ref, hi_ref, hg_ref, s0_ref, lb_ref, hgw_ref, ltri_ref, llast_ref, cm_ref, bd_ref,
                        y_ref, st_ref):
    hi = hi_ref[...]
    q_dec, k_dec, k_rem, glast = _hgrn_rows(hq_ref[...], hf_ref[...], hi, lb_ref[...], ltri_ref[...], llast_ref[...])
    cmask = cm_ref[...] > 0
    outs = []
    for p in range(N_PAIRS):
        def get_state(c, p=p):
            return s0_ref[c, p]

        def put_state(c, val, p=p):
            st_ref[c, p] = val

        outs.append(_hgrn_pair(p, q_dec, k_dec, k_rem, glast, hi, cmask, DEC_CHUNK, get_state, put_state))
    o = jnp.concatenate(outs, axis=1)
    y_ref[...] = _hgrn_finish(o, hg_ref[...], bd_ref[...], hgw_ref[...])


def _hgrn_decode(hq, hf, hi, hg, s0, lb, hgw, masks, bd):
    n = hq.shape[0]
    ltri, llast, cm = masks
    full = lambda a: pl.BlockSpec(a.shape, lambda i: (0,) * a.ndim)
    args = (hq, hf, hi, hg, s0, lb, hgw, ltri, llast, cm, bd)
    return pl.pallas_call(
        _hgrn_decode_kernel,
        grid=(1,),
        in_specs=[full(a) for a in args],
        out_specs=[pl.BlockSpec((n, WIDTH), lambda i: (0, 0)), full(s0)],
        out_shape=[jax.ShapeDtypeStruct((n, WIDTH), BF16), jax.ShapeDtypeStruct(s0.shape, F32)],
        compiler_params=_cparams(("arbitrary",)),
        name="hgrn_decode",
    )(*args)


def _chunk_masks(rows, chunk, valid):
    r = jnp.arange(rows)
    same = (r[:, None] // chunk) == (r[None, :] // chunk)
    ltri = same & (r[None, :] <= r[:, None])
    llast = same & ((r[None, :] % chunk) < valid)
    return ltri.astype(BF16), llast.astype(BF16), ltri.astype(BF16)


def _mixout_kernel(x_ref, osb_ref, yb_ref, ga_ref, gb_ref, wa_ref, wb_ref, wo_ref, n2_ref, wr_ref, br_ref,
                   x1_ref, eid_ref, ew_ref):
    ya = _dot(osb_ref[...], wa_ref[...])
    yb = _dot(yb_ref[...], wb_ref[...])
    merged = jax.nn.sigmoid(ga_ref[...]) * ya + jax.nn.sigmoid(gb_ref[...]) * yb
    x1 = x_ref[...] + _dot(merged.astype(BF16), wo_ref[...])
    x1_ref[...] = x1
    h2 = _rms_rows(x1, n2_ref[...]).astype(BF16)
    lg = _dot_nt(wr_ref[...], h2) + br_ref[...]
    tm = lg.shape[1]
    gl = lg[0:8]
    gmax = jnp.max(gl, axis=0, keepdims=True)
    p_group = 1.0 / jnp.sum(jnp.exp(gl - gmax), axis=0, keepdims=True)
    gr = lax.broadcasted_iota(I32, (8, tm), 0)
    gidx = jnp.min(jnp.where(gl == gmax, gr, 8), axis=0, keepdims=True)
    el = lg[16:16 + N_EXPERTS]
    er = lax.broadcasted_iota(I32, (N_EXPERTS, tm), 0)
    e1 = jnp.where((er >> 3) == gidx, el, NEG)
    m1 = jnp.max(e1, axis=0, keepdims=True)
    i1 = jnp.min(jnp.where(e1 == m1, er, N_EXPERTS), axis=0, keepdims=True)
    e2 = jnp.where(er == i1, NEG, e1)
    m2 = jnp.max(e2, axis=0, keepdims=True)
    i2 = jnp.min(jnp.where(e2 == m2, er, N_EXPERTS), axis=0, keepdims=True)
    t = jnp.exp(m2 - m1)
    w1 = p_group / (1.0 + t)
    eid_ref[...] = jnp.concatenate([i1, i2], axis=0)
    ew_ref[...] = jnp.concatenate([w1, w1 * t], axis=0)


def _mix_out(x2d, osb, yb, ga, gb, wa, wb, wo, n2, wr, br):
    n = x2d.shape[0]
    tm = min(ROW_TILE, n)
    row = lambda w: pl.BlockSpec((tm, w), lambda i: (i, 0))
    full = lambda a: pl.BlockSpec(a.shape, lambda i: (0,) * a.ndim)
    return pl.pallas_call(
        _mixout_kernel,
        grid=(n // tm,),
        in_specs=[row(D_MODEL), row(WIDTH), row(WIDTH), row(D_MODEL), row(D_MODEL),
                  full(wa), full(wb), full(wo), full(n2), full(wr), full(br)],
        out_specs=[row(D_MODEL), pl.BlockSpec((2, tm), lambda i: (0, i)), pl.BlockSpec((2, tm), lambda i: (0, i))],
        out_shape=[jax.ShapeDtypeStruct((n, D_MODEL), F32), jax.ShapeDtypeStruct((2, n), I32),
                   jax.ShapeDtypeStruct((2, n), F32)],
        compiler_params=_cparams(("parallel",)),
        name="mix_out",
    )(x2d, osb, yb, ga, gb, wa, wb, wo, n2, wr, br)


def _expert_kernel(texp_ref, tval_ref, nused_ref, tok_ref, tokn_ref, dst_ref,
                   x1_hbm, n2_ref, roww_ref, wg_ref, wu_ref, wd_ref, out_hbm,
                   xbuf, ybuf, gsem, ssem):
    i = pl.program_id(0)
    nt = pl.num_programs(0)
    nused = nused_ref[0]
    slot = lax.rem(i, 2)
    tm = ybuf.shape[0]

    def gather_copy(idx_ref, r, sl):
        return pltpu.make_async_copy(x1_hbm.at[pl.ds(idx_ref[0, 0, r], 1)], xbuf.at[sl, pl.ds(r, 1)], gsem.at[sl])

    def start_gather(idx_ref, sl):
        def body(r, _):
            gather_copy(idx_ref, r, sl).start()
            return 0
        lax.fori_loop(0, tm, body, 0, unroll=8)

    def scatter_copy(r):
        return pltpu.make_async_copy(ybuf.at[pl.ds(r, 1)], out_hbm.at[pl.ds(dst_ref[0, 0, r], 1)], ssem.at[0])

    def wait_scatter(count):
        def body(r, _):
            pltpu.make_async_copy(ybuf.at[pl.ds(0, 1)], out_hbm.at[pl.ds(0, 1)], ssem.at[0]).wait()
            return 0
        lax.fori_loop(0, count, body, 0)

    @pl.when(i == 0)
    def _():
        start_gather(tok_ref, 0)

    @pl.when(i + 1 < nused)
    def _():
        start_gather(tokn_ref, 1 - slot)

    @pl.when(i < nused)
    def _():
        def wbody(r, _):
            gather_copy(tok_ref, r, slot).wait()
            return 0
        lax.fori_loop(0, tm, wbody, 0, unroll=8)
        h = _rms_rows(xbuf[slot], n2_ref[...]).astype(BF16)
        a = _dot(h, wg_ref[0])
        u = _dot(h, wu_ref[0])
        hb = (a * jax.nn.sigmoid(a) * u).astype(BF16)
        y = _dot(hb, wd_ref[0]) * roww_ref[...]

        @pl.when(i >= 1)
        def _():
            wait_scatter(tval_ref[jnp.maximum(i - 1, 0)])

        ybuf[...] = y

        def sbody(r, _):
            scatter_copy(r).start()
            return 0
        lax.fori_loop(0, tval_ref[i], sbody, 0)

        @pl.when(i == nt - 1)
        def _():
            wait_scatter(tval_ref[i])

    @pl.when(i == nused)
    def _():
        wait_scatter(tval_ref[jnp.maximum(i - 1, 0)])


def _experts(x1, n2, meta, wg, wu, wd):
    n = x1.shape[0]
    texp, tval, nused, row_tok, row_dst, row_w = meta
    nt = texp.shape[0]
    tm = MOE_TILE
    tok3 = row_tok.reshape(nt, 1, tm)
    dst3 = row_dst.reshape(nt, 1, tm)
    smem_blk = lambda f: pl.BlockSpec((1, 1, tm), f, memory_space=pltpu.SMEM)
    return pl.pallas_call(
        _expert_kernel,
        grid_spec=pltpu.PrefetchScalarGridSpec(
            num_scalar_prefetch=3,
            grid=(nt,),
            in_specs=[smem_blk(lambda i, te, tv, nu: (i, 0, 0)),
                      smem_blk(lambda i, te, tv, nu: (jnp.minimum(i + 1, nt - 1), 0, 0)),
                      smem_blk(lambda i, te, tv, nu: (i, 0, 0)),
                      pl.BlockSpec(memory_space=pl.ANY),
                      pl.BlockSpec((1, D_MODEL), lambda i, te, tv, nu: (0, 0)),
                      pl.BlockSpec((tm, 1), lambda i, te, tv, nu: (i, 0)),
                      pl.BlockSpec((1, D_MODEL, D_FF), lambda i, te, tv, nu: (te[i], 0, 0)),
                      pl.BlockSpec((1, D_MODEL, D_FF), lambda i, te, tv, nu: (te[i], 0, 0)),
                      pl.BlockSpec((1, D_FF, D_MODEL), lambda i, te, tv, nu: (te[i], 0, 0))],
            out_specs=pl.BlockSpec(memory_space=pl.ANY),
            scratch_shapes=[pltpu.VMEM((2, tm, D_MODEL), F32),
                            pltpu.VMEM((tm, D_MODEL), F32),
                            pltpu.SemaphoreType.DMA((2,)),
                            pltpu.SemaphoreType.DMA((1,))],
        ),
        out_shape=jax.ShapeDtypeStruct((2 * n, D_MODEL), F32),
        compiler_params=_cparams(("arbitrary",)),
        name="experts",
    )(texp, tval, nused, tok3, tok3, dst3, x1, n2, row_w, wg, wu, wd)


def _moe_metadata(eid, n_tok):
    tm = MOE_TILE
    n_assign = 2 * n_tok
    e_flat = eid.reshape(-1)
    order = jnp.argsort(e_flat).astype(I32)
    e_sorted = e_flat[order]
    counts = jnp.sum((e_flat[:, None] == jnp.arange(N_EXPERTS, dtype=I32)[None, :]).astype(I32), axis=0)
    starts = jnp.cumsum(counts) - counts
    padded = ((counts + tm - 1) // tm) * tm
    pad_ends = jnp.cumsum(padded)
    pad_starts = pad_ends - padded
    dest = pad_starts[e_sorted] + (jnp.arange(n_assign, dtype=I32) - starts[e_sorted])
    n_tiles = n_assign // tm + N_EXPERTS
    n_rows = n_tiles * tm
    row_asg = jnp.zeros((n_rows,), I32).at[dest].set(order)
    row_tok = jnp.where(row_asg >= n_tok, row_asg - n_tok, row_asg)
    tile_lo = jnp.arange(n_tiles, dtype=I32) * tm
    texp = jnp.clip(jnp.searchsorted(pad_ends, tile_lo, side="right"), 0, N_EXPERTS - 1).astype(I32)
    tval = jnp.clip(counts[texp] - (tile_lo - pad_starts[texp]), 0, tm).astype(I32)
    nused = (pad_ends[-1:] // tm).astype(I32)
    return texp, tval, nused, row_tok, row_asg, dest, order, n_rows


def _combine_kernel(x_ref, a_ref, b_ref, o_ref):
    o_ref[...] = x_ref[...] + (a_ref[...] + b_ref[...])


def _combine(x1, buf):
    n = x1.shape[0]
    tm = min(2 * ROW_TILE, n)
    nb = n // tm
    return pl.pallas_call(
        _combine_kernel,
        grid=(nb,),
        in_specs=[pl.BlockSpec((tm, D_MODEL), lambda i: (i, 0)),
                  pl.BlockSpec((tm, D_MODEL), lambda i: (i, 0)),
                  pl.BlockSpec((tm, D_MODEL), lambda i: (i + nb, 0))],
        out_specs=pl.BlockSpec((tm, D_MODEL), lambda i: (i, 0)),
        out_shape=jax.ShapeDtypeStruct((n, D_MODEL), F32),
        compiler_params=_cparams(("parallel",)),
        name="combine",
    )(x1, buf, buf)


def _moe(x1, n2, eid, ew, wg, wu, wd):
    n = x1.shape[0]
    texp, tval, nused, row_tok, row_asg, dest, order, n_rows = _moe_metadata(eid, n)
    row_w = jnp.zeros((n_rows,), F32).at[dest].set(ew.reshape(-1)[order]).reshape(n_rows, 1)
    buf = _experts(x1, n2, (texp, tval, nused, row_tok, row_asg, row_w), wg, wu, wd)
    return _combine(x1, buf)


def _state_to_pairs(s):
    b = s.shape[0]
    st = jnp.swapaxes(s, -1, -2).reshape(b, N_PAIRS, 2, HEAD_DIM, HEAD_DIM)
    z = jnp.zeros_like(st[:, :, 0])
    top = jnp.concatenate([st[:, :, 0], z], axis=-1)
    bot = jnp.concatenate([z, st[:, :, 1]], axis=-1)
    return jnp.concatenate([top, bot], axis=-2)


def _pairs_to_state(st):
    b = st.shape[0]
    h0 = st[:, :, :HEAD_DIM, :HEAD_DIM]
    h1 = st[:, :, HEAD_DIM:, HEAD_DIM:]
    s = jnp.stack([h0, h1], axis=2).reshape(b, N_HEADS, HEAD_DIM, HEAD_DIM)
    return jnp.swapaxes(s, -1, -2)


def kernel(x_prompt, x_sample, cache_k, cache_v, page_table, state_hgrn, norm1_w, w_in, q_norm_w, k_norm_w, sb_bias,
           lb_logits, hg_norm_w, w_branch_a, w_branch_b, w_out, norm2_w, w_group_router, b_group_router,
           w_expert_router, b_expert_router, w_gate_exp, w_up_exp, w_down_exp):
    depth = w_in.shape[0]
    assert depth == 1, "single layer"
    l = 0
    b, t, _ = x_prompt.shape
    bd_, tq, _ = x_sample.shape
    n_p = b * t
    n_s = bd_ * tq
    assert t % HG_CHUNK == 0 and tq <= DEC_CHUNK and tq % HG_CHUNK != 0

    lbs = jnp.cumsum(jax.nn.softmax(lb_logits.astype(F32), axis=0), axis=0)
    lb = lbs[l].reshape(1, WIDTH)
    n1 = norm1_w[l].reshape(1, D_MODEL)
    n2 = norm2_w[l].reshape(1, D_MODEL)
    w_in_b = w_in[l].astype(BF16)
    qw = jnp.tile(q_norm_w[l], N_HEADS).reshape(1, WIDTH)
    kw = jnp.tile(k_norm_w[l], N_HEADS).reshape(1, WIDTH)
    hgw = jnp.tile(hg_norm_w[l], N_HEADS).reshape(1, WIDTH)
    hd = jnp.arange(WIDTH) // HEAD_DIM
    bdiag = (hd[:, None] == hd[None, :]).astype(BF16)
    bias = sb_bias[l].astype(F32)
    wa = w_branch_a[l].astype(BF16)
    wb = w_branch_b[l].astype(BF16)
    wo = w_out[l].astype(BF16)
    wr = jnp.zeros((ROUTER_ROWS, D_MODEL), F32)
    wr = wr.at[0:N_GROUPS].set(w_group_router[l].T).at[16:16 + N_EXPERTS].set(w_expert_router[l].T).astype(BF16)
    br = jnp.full((ROUTER_ROWS, 1), NEG, F32)
    br = br.at[0:N_GROUPS, 0].set(b_group_router[l]).at[16:16 + N_EXPERTS, 0].set(b_expert_router[l])
    wg = w_gate_exp[l].astype(BF16)
    wu = w_up_exp[l].astype(BF16)
    wd = w_down_exp[l].astype(BF16)

    def tri_ge(n):
        r = jnp.arange(n)
        return (r[:, None] >= r[None, :]).astype(BF16)

    xp = x_prompt.reshape(n_p, D_MODEL)
    q, k, kb, v, vb, hq, hf, hi, hg, ga, gb = _in_proj(xp, n1, w_in_b, bdiag, qw, kw)
    sb_tile = min(SB_TILE, t)
    o_sb = _sb_prompt(q.reshape(b, t, WIDTH), kb.reshape(b, t, WIDTH), vb.reshape(b, t, WIDTH), bias, tri_ge(sb_tile))
    r3 = lambda a: a.reshape(b, t, WIDTH)
    y_b, st_p = _hgrn_prompt(r3(hq), r3(hf), r3(hi), r3(hg), lb, hgw,
                             _chunk_masks(min(HG_ROWS, t), HG_CHUNK, HG_CHUNK), bdiag)
    x1, eid, ew = _mix_out(xp, o_sb.reshape(n_p, WIDTH), y_b.reshape(n_p, WIDTH), ga, gb, wa, wb, wo, n2, wr, br)
    y_prompt = _moe(x1, n2, eid, ew, wg, wu, wd).reshape(b, t, D_MODEL)
    new_k_prompt = k.reshape(1, b, t, N_HEADS, HEAD_DIM)
    new_v_prompt = v.reshape(1, b, t, N_HEADS, HEAD_DIM)
    new_hgrn_prompt = _pairs_to_state(st_p)[None]

    xs = x_sample.reshape(n_s, D_MODEL)
    q, k, kb, v, vb, hq, hf, hi, hg, ga, gb = _in_proj(xs, n1, w_in_b, bdiag, qw, kw)
    pad_keys = lambda a: jnp.pad(a.reshape(bd_, tq, WIDTH), ((0, 0), (0, LANES - tq), (0, 0)))
    n_pool = cache_k.shape[1]
    o_sb = _sb_decode(q.reshape(bd_, tq, WIDTH), pad_keys(kb), pad_keys(vb),
                      cache_k[l].reshape(n_pool, PAGE, WIDTH), cache_v[l].reshape(n_pool, PAGE, WIDTH),
                      page_table, bias, tri_ge(2 * PAGE), tri_ge(LANES))
    pad_rows = lambda a: jnp.pad(a.reshape(bd_, tq, WIDTH), ((0, 0), (0, DEC_CHUNK - tq), (0, 0))).reshape(
        bd_ * DEC_CHUNK, WIDTH)
    y_b, st_s = _hgrn_decode(pad_rows(hq), pad_rows(hf), pad_rows(hi), pad_rows(hg),
                             _state_to_pairs(state_hgrn[l].astype(F32)), lb, hgw,
                             _chunk_masks(bd_ * DEC_CHUNK, DEC_CHUNK, tq), bdiag)
    y_b = y_b.reshape(bd_, DEC_CHUNK, WIDTH)[:, :tq].reshape(n_s, WIDTH)
    x1, eid, ew = _mix_out(xs, o_sb.reshape(n_s, WIDTH).astype(BF16), y_b, ga, gb, wa, wb, wo, n2, wr, br)
    y_sample = _moe(x1, n2, eid, ew, wg, wu, wd).reshape(bd_, tq, D_MODEL)
    new_k_sample = k.reshape(1, bd_, tq, N_HEADS, HEAD_DIM)
    new_v_sample = v.reshape(1, bd_, tq, N_HEADS, HEAD_DIM)
    new_hgrn_sample = _pairs_to_state(st_s)[None]

    return (y_prompt, y_sample, new_k_prompt, new_v_prompt, new_hgrn_prompt,
            new_k_sample, new_v_sample, new_hgrn_sample)
```

```python
import functools

import jax
import jax.numpy as jnp
import numpy as np
from jax import lax
from jax.experimental import pallas as pl
from jax.experimental.pallas import tpu as pltpu

F32 = jnp.float32
BF16 = jnp.bfloat16
I32 = jnp.int32

D_MODEL = 1024
N_HEADS = 8
HEAD_DIM = 64
WIDTH = N_HEADS * HEAD_DIM
LANES = 128
N_PAIRS = WIDTH // LANES
N_GROUPS = 4
EXPERTS_PER_GROUP = 8
N_EXPERTS = N_GROUPS * EXPERTS_PER_GROUP
D_FF = 512
PAGE = 128
HG_CHUNK = 32
RMS_EPS = 1e-6
NEG = -1e30
LOG2E = 1.4426950408889634

VMEM_LIMIT = 56 * 1024 * 1024
ROW_TILE = 256
SB_TILE = 256
SB_PAIRS = 2
DEC_GROUP = 8
HG_ROWS = 256
MOE_TILE = 256
MOE_TILE_SMALL = 32
PAGES_PER_STEP = 16
DEC_CHUNK = 16
ROUTER_ROWS = 48


def _cparams(sem):
    return pltpu.CompilerParams(dimension_semantics=sem, vmem_limit_bytes=VMEM_LIMIT)


def _dot(a, b):
    return jnp.dot(a, b, preferred_element_type=F32)


def _dot_nt(a, b):
    return lax.dot_general(a, b, (((1,), (1,)), ((), ())), preferred_element_type=F32)


def _split_bf16(x):
    hi = x.astype(BF16)
    lo = (x - hi.astype(F32)).astype(BF16)
    return hi, lo


def _prep(x, precise):
    return _split_bf16(x) if precise else (x.astype(BF16),)


def _mm(pa, pb, nt=False):
    d = _dot_nt if nt else _dot
    out = d(pa[0], pb[0])
    if len(pa) == 2:
        out = out + (d(pa[0], pb[1]) + d(pa[1], pb[0]))
    return out


def _dot_split(x, m):
    hi, lo = _split_bf16(x)
    return _dot(hi, m) + _dot(lo, m)


def _head_mean_sq(h, bd):
    return _dot_split(h * h, bd) * (1.0 / HEAD_DIM)


def _rms_rows(x, w):
    ms = jnp.mean(x * x, axis=-1, keepdims=True)
    return x * lax.rsqrt(ms + RMS_EPS) * w


def _softplus2(u):
    return jnp.maximum(u, 0.0) + jnp.log2(1.0 + jnp.exp2(-jnp.abs(u)))


def _cum_ge(sp, tri2):
    hi, lo = _split_bf16(sp)
    return _dot(jnp.concatenate([hi, lo], axis=1), tri2)


_SEGS = {"q": (0, 512), "k": (512, 1024), "v": (1024, 1536), "hq": (1536, 2048), "hf": (2048, 2560),
         "hi": (2560, 3072), "hg": (3072, 3584), "ga": (3584, 4608), "gb": (4608, 5632)}
D_IN = 5632


def _inproj_kernel(x_ref, n1_ref, w_ref, bd_ref, qw_ref, kw_ref,
                   q_ref, k_ref, kb_ref, v_ref, vb_ref, hq_ref, hf_ref, hi_ref, hg_ref, ga_ref, gb_ref, *, kv_t):
    xn = _rms_rows(x_ref[...], n1_ref[...]).astype(BF16)
    bd = bd_ref[...]

    def seg(name):
        a, b = _SEGS[name]
        return _dot(xn, w_ref[:, a:b])

    hq = seg("q")
    hk = seg("k")
    hv = seg("v")
    hq_ref[...] = seg("hq")
    hf_ref[...] = seg("hf")
    hi_ref[...] = seg("hi")
    hg_ref[...] = seg("hg")
    ga_ref[...] = seg("ga")
    gb_ref[...] = seg("gb")
    qn = hq * lax.rsqrt(_head_mean_sq(hq, bd) + RMS_EPS) * qw_ref[...]
    q_ref[...] = (qn * (HEAD_DIM ** -0.5 * LOG2E)).astype(BF16)
    kn = hk * lax.rsqrt(_head_mean_sq(hk, bd) + RMS_EPS) * kw_ref[...]
    kb_ref[...] = kn.astype(BF16)
    vb_ref[...] = hv.astype(BF16)
    if kv_t:
        k_ref[0] = kn.T
        v_ref[0] = hv.T
    else:
        k_ref[...] = kn
        v_ref[...] = hv


def _in_proj(x2d, n1, w_in, bd, qw, kw, t_len=None):
    n = x2d.shape[0]
    tm = min(ROW_TILE, n)
    row = lambda w: pl.BlockSpec((tm, w), lambda i: (i, 0))
    full = lambda a: pl.BlockSpec(a.shape, lambda i: (0,) * a.ndim)
    if t_len is None:
        kv_spec = row(WIDTH)
        kv_shape = jax.ShapeDtypeStruct((n, WIDTH), F32)
    else:
        per_seq = t_len // tm
        kv_spec = pl.BlockSpec((1, WIDTH, tm), lambda i: (i // per_seq, 0, i % per_seq))
        kv_shape = jax.ShapeDtypeStruct((n // t_len, WIDTH, t_len), F32)
    rows = lambda w, dt: (row(w), jax.ShapeDtypeStruct((n, w), dt))
    outs = [rows(WIDTH, BF16), (kv_spec, kv_shape), rows(WIDTH, BF16), (kv_spec, kv_shape), rows(WIDTH, BF16),
            rows(WIDTH, F32), rows(WIDTH, F32), rows(WIDTH, F32), rows(WIDTH, F32),
            rows(D_MODEL, F32), rows(D_MODEL, F32)]
    return pl.pallas_call(
        functools.partial(_inproj_kernel, kv_t=t_len is not None),
        grid=(n // tm,),
        in_specs=[row(D_MODEL), full(n1), full(w_in), full(bd), full(qw), full(kw)],
        out_specs=[s for s, _ in outs],
        out_shape=[s for _, s in outs],
        compiler_params=_cparams(("parallel",)),
        name="in_proj",
    )(x2d, n1, w_in, bd, qw, kw)


def _inproj_precise_kernel(x_ref, n1_ref, w_ref, bd_ref, qw_ref, kw_ref, h_ref):
    j = pl.program_id(0)
    xn = _rms_rows(x_ref[...], n1_ref[...])
    h = _mm(_split_bf16(xn), _split_bf16(w_ref[...]))
    bd = bd_ref[...]

    @pl.when(j == 0)
    def _():
        h_ref[...] = h * lax.rsqrt(_head_mean_sq(h, bd) + RMS_EPS) * qw_ref[...] * (HEAD_DIM ** -0.5 * LOG2E)

    @pl.when(j == 1)
    def _():
        h_ref[...] = h * lax.rsqrt(_head_mean_sq(h, bd) + RMS_EPS) * kw_ref[...]

    @pl.when(j >= 2)
    def _():
        h_ref[...] = h


def _in_proj_precise(x2d, n1, w_in, bd, qw, kw):
    n = x2d.shape[0]
    full = lambda a: pl.BlockSpec(a.shape, lambda j: (0,) * a.ndim)
    h = pl.pallas_call(
        _inproj_precise_kernel,
        grid=(D_IN // WIDTH,),
        in_specs=[full(x2d), full(n1), pl.BlockSpec((D_MODEL, WIDTH), lambda j: (0, j)), full(bd), full(qw), full(kw)],
        out_specs=pl.BlockSpec((n, WIDTH), lambda j: (0, j)),
        out_shape=jax.ShapeDtypeStruct((n, D_IN), F32),
        compiler_params=_cparams(("arbitrary",)),
        name="in_proj_precise",
    )(x2d, n1, w_in, bd, qw, kw)
    return [h[:, a:b] for a, b in _SEGS.values()]


def _sb_prompt_kernel(bias_ref, q_ref, k_ref, v_ref, tri2_ref, o_ref, *, tile):
    p = pl.program_id(1)
    i = pl.program_id(2)
    nh = 2 * SB_PAIRS
    lane = lax.broadcasted_iota(I32, (tile, LANES), 1)
    first = lane < HEAD_DIM
    qs = []
    for pp in range(SB_PAIRS):
        q = q_ref[0, :, pp * LANES:(pp + 1) * LANES]
        zero = jnp.zeros_like(q)
        qs += [jnp.where(first, q, zero), jnp.where(first, zero, q)]
    bias = [bias_ref[nh * p + h] for h in range(nh)]
    tri2 = tri2_ref[...]
    row = lax.broadcasted_iota(I32, (tile, tile), 0)
    col = lax.broadcasted_iota(I32, (tile, tile), 1)
    causal = col < row

    def step(j, carry, accs, diag):
        start = pl.multiple_of(j * tile, tile)
        pair = lambda h: slice((h // 2) * LANES, (h // 2 + 1) * LANES)
        us = [_dot_nt(qs[h], k_ref[0, pl.ds(start, tile), pair(h)]) + bias[h] for h in range(nh)]
        cums = []
        for h in range(nh):
            sp = _softplus2(us[h])
            if diag:
                sp = jnp.where(causal, sp, 0.0)
            cums.append(_cum_ge(sp, tri2) + carry[h])
        new_a = []
        for h in range(nh):
            a = jnp.exp2(us[h] - cums[h])
            if diag:
                a = jnp.where(causal, a, 0.0)
            new_a.append(accs[h] + _dot(a.astype(BF16), v_ref[0, pl.ds(start, tile), pair(h)]))
        return tuple(c[:, 0:1] for c in cums), tuple(new_a)

    zc = tuple(jnp.zeros((tile, 1), F32) for _ in range(nh))
    za = tuple(jnp.zeros((tile, LANES), F32) for _ in range(nh))
    carry, accs = step(i, zc, za, True)

    def body(n, state):
        return step(i - 1 - n, state[0], state[1], False)

    carry, accs = lax.fori_loop(0, i, body, (carry, accs))
    for pp in range(SB_PAIRS):
        o_ref[0, :, pp * LANES:(pp + 1) * LANES] = jnp.where(first, accs[2 * pp], accs[2 * pp + 1]).astype(BF16)


def _sb_prompt(q, k, v, bias2, tri2):
    b, t, _ = q.shape
    tile = tri2.shape[1]
    w = SB_PAIRS * LANES
    kern = functools.partial(_sb_prompt_kernel, tile=tile)
    return pl.pallas_call(
        kern,
        grid_spec=pltpu.PrefetchScalarGridSpec(
            num_scalar_prefetch=0,
            grid=(b, WIDTH // w, t // tile),
            in_specs=[pl.BlockSpec(memory_space=pltpu.SMEM),
                      pl.BlockSpec((1, tile, w), lambda bi, p, i: (bi, i, p)),
                      pl.BlockSpec((1, t, w), lambda bi, p, i: (bi, 0, p)),
                      pl.BlockSpec((1, t, w), lambda bi, p, i: (bi, 0, p)),
                      pl.BlockSpec((2 * tile, tile), lambda bi, p, i: (0, 0))],
            out_specs=pl.BlockSpec((1, tile, w), lambda bi, p, i: (bi, i, p)),
        ),
        out_shape=jax.ShapeDtypeStruct((b, t, WIDTH), BF16),
        compiler_params=_cparams(("parallel", "parallel", "arbitrary")),
        name="sb_prompt",
    )(bias2, q, k, v, tri2)


def _sb_decode_kernel(pt_ref, bias_ref, q_ref, knt_ref, vnt_ref, ck_hbm, cv_hbm, tri2_ref, o_ref,
                      kbuf, vbuf, sem, qbd_s, bias_s, acc_s, carry_s, *, n_pages, n_q):
    tri2 = tri2_ref[...]
    b = pl.program_id(0)
    s = pl.program_id(1)
    nb = pl.num_programs(0)
    ns = pl.num_programs(1)
    pp = kbuf.shape[1]
    g = b * ns + s
    slot = lax.rem(g, 2)
    rows = n_q * N_HEADS

    def page_copies(bb, ss, sl, i):
        page = pt_ref[bb, n_pages - (ss + 1) * pp + i]
        return (pltpu.make_async_copy(ck_hbm.at[page], kbuf.at[sl, i], sem.at[0, sl]),
                pltpu.make_async_copy(cv_hbm.at[page], vbuf.at[sl, i], sem.at[1, sl]))

    def issue(bb, ss, sl):
        for i in range(pp):
            ck, cv = page_copies(bb, ss, sl, i)
            ck.start()
            cv.start()

    @pl.when(g == 0)
    def _():
        issue(0, 0, 0)

    @pl.when(g + 1 < nb * ns)
    def _():
        last = s + 1 == ns
        issue(jnp.where(last, b + 1, b), jnp.where(last, 0, s + 1), 1 - slot)

    head_of_lane = lax.broadcasted_iota(I32, (N_HEADS, WIDTH), 1) >> 6
    head_of_row = lax.broadcasted_iota(I32, (N_HEADS, WIDTH), 0)
    head_mask = head_of_lane == head_of_row

    def tall(a2, pb, nt):
        d = _dot_nt if nt else _dot
        both = d(a2, pb[0])
        return both[:rows] + both[rows:] + d(a2[:rows], pb[1])

    @pl.when(s == 0)
    def _():
        qf = q_ref[0]
        for qi in range(n_q):
            blk = jnp.where(head_mask, jnp.broadcast_to(qf[qi:qi + 1, :], (N_HEADS, WIDTH)), 0.0)
            hi, lo = _split_bf16(blk)
            qbd_s[qi * N_HEADS:(qi + 1) * N_HEADS, :] = hi
            qbd_s[rows + qi * N_HEADS:rows + (qi + 1) * N_HEADS, :] = lo
        r = lax.broadcasted_iota(I32, (rows, LANES), 0) & (N_HEADS - 1)
        bv = jnp.zeros((rows, LANES), F32)
        for h in range(N_HEADS):
            bv = jnp.where(r == h, bias_ref[h], bv)
        bias_s[...] = bv
        u = tall(qbd_s[...], _split_bf16(knt_ref[0]), False) + bv
        key = lax.broadcasted_iota(I32, (rows, LANES), 1)
        qidx = lax.broadcasted_iota(I32, (rows, LANES), 0) >> 3
        mask = key < qidx
        cum = _cum_ge(jnp.where(mask, _softplus2(u), 0.0), tri2)
        a = jnp.where(mask, jnp.exp2(u - cum), 0.0)
        acc_s[...] = tall(jnp.concatenate(_split_bf16(a), axis=0), _split_bf16(vnt_ref[0]), True)
        carry_s[...] = jnp.broadcast_to(cum[:, 0:1], (rows, LANES))

    for i in range(pp):
        ck, cv = page_copies(b, s, slot, i)
        ck.wait()
        cv.wait()

    qbd = qbd_s[...]
    bias = bias_s[...]
    group = min(DEC_GROUP, pp)

    def body(n, _):
        js = [(pp // group - 1 - n) * group + (group - 1 - t) for t in range(group)]
        us = [tall(qbd, _split_bf16(kbuf[slot, j]), False) + bias for j in js]
        sums = [_cum_ge(_softplus2(u), tri2) for u in us]
        carry = carry_s[...]
        acc = acc_s[...]
        for t, j in enumerate(js):
            cum = sums[t] + carry
            a = jnp.exp2(us[t] - cum)
            acc = acc + tall(jnp.concatenate(_split_bf16(a), axis=0), _split_bf16(vbuf[slot, j]), True)
            carry = jnp.broadcast_to(cum[:, 0:1], (rows, LANES))
        acc_s[...] = acc
        carry_s[...] = carry
        return 0

    lax.fori_loop(0, pp // group, body, 0)

    @pl.when(s == ns - 1)
    def _():
        acc = acc_s[...]
        for qi in range(n_q):
            blk = jnp.where(head_mask, acc[qi * N_HEADS:(qi + 1) * N_HEADS, :], 0.0)
            o_ref[0, qi:qi + 1, :] = jnp.sum(blk, axis=0, keepdims=True)


def _sb_decode(q, knt, vnt, ckt, cvt, page_table, bias2, tri2):
    bd, n_q, _ = q.shape
    n_pages = page_table.shape[1]
    pp = min(PAGES_PER_STEP, n_pages)
    rows = n_q * N_HEADS
    kern = functools.partial(_sb_decode_kernel, n_pages=n_pages, n_q=n_q)
    return pl.pallas_call(
        kern,
        grid_spec=pltpu.PrefetchScalarGridSpec(
            num_scalar_prefetch=1,
            grid=(bd, n_pages // pp),
            in_specs=[pl.BlockSpec(memory_space=pltpu.SMEM),
                      pl.BlockSpec((1, n_q, WIDTH), lambda b, s, pt: (b, 0, 0)),
                      pl.BlockSpec((1, WIDTH, PAGE), lambda b, s, pt: (b, 0, 0)),
                      pl.BlockSpec((1, WIDTH, PAGE), lambda b, s, pt: (b, 0, 0)),
                      pl.BlockSpec(memory_space=pl.ANY),
                      pl.BlockSpec(memory_space=pl.ANY),
                      pl.BlockSpec((2 * PAGE, PAGE), lambda b, s, pt: (0, 0))],
            out_specs=pl.BlockSpec((1, n_q, WIDTH), lambda b, s, pt: (b, 0, 0)),
            scratch_shapes=[pltpu.VMEM((2, pp, WIDTH, PAGE), F32),
                            pltpu.VMEM((2, pp, WIDTH, PAGE), F32),
                            pltpu.SemaphoreType.DMA((2, 2)),
                            pltpu.VMEM((2 * rows, WIDTH), BF16),
                            pltpu.VMEM((rows, LANES), F32),
                            pltpu.VMEM((rows, WIDTH), F32),
                            pltpu.VMEM((rows, LANES), F32)],
        ),
        out_shape=jax.ShapeDtypeStruct((bd, n_q, WIDTH), F32),
        compiler_params=_cparams(("arbitrary", "arbitrary")),
        name="sb_decode",
    )(page_table, bias2, q, knt, vnt, ckt, cvt, tri2)


def _hgrn_rows(hq, hf, hi, lb, ltri, llast):
    f = lb + (1.0 - lb) * jax.nn.sigmoid(hf)
    g = jnp.log(f)
    kk = 1.0 - f
    g_hi, g_lo = _split_bf16(g)
    gcum = _dot(ltri, g_hi) + _dot(ltri, g_lo)
    glast = _dot(llast, g_hi) + _dot(llast, g_lo)
    q_dec = hq * jnp.exp(gcum)
    k_dec = kk * jnp.exp(-gcum)
    k_rem = kk * jnp.exp(glast - gcum)
    return q_dec, k_dec, k_rem, glast


def _hgrn_block(q_dec, k_dec, k_rem, glast, hi, cmask, chunk, chained, get_state, put_state, precise=False):
    n_rows = q_dec.shape[0]
    n_chunks = n_rows // chunk
    pairs = [slice(p * LANES, (p + 1) * LANES) for p in range(N_PAIRS)]
    first = lax.broadcasted_iota(I32, (n_rows, LANES), 1) < HEAD_DIM
    row_chunk = lax.broadcasted_iota(I32, (n_rows, LANES), 0) >> (chunk.bit_length() - 1)
    sr = lax.broadcasted_iota(I32, (LANES, LANES), 0) >> 6
    sc = lax.broadcasted_iota(I32, (LANES, LANES), 1) >> 6
    same_head = sr == sc
    prep = lambda x: _prep(x, precise)
    vbs = [prep(hi[:, sl]) for sl in pairs]
    scores = []
    for sl in pairs:
        q = q_dec[:, sl]
        kb = prep(k_dec[:, sl])
        s0 = _mm(prep(jnp.where(first, q, 0.0)), kb, nt=True)
        s1 = _mm(prep(jnp.where(first, 0.0, q)), kb, nt=True)
        scores.append((prep(jnp.where(cmask, s0, 0.0)), prep(jnp.where(cmask, s1, 0.0))))
    intra = [jnp.where(first, _mm(s0, vb), _mm(s1, vb)) for (s0, s1), vb in zip(scores, vbs)]
    upds = []
    for sl in pairs:
        vt = prep(hi[:, sl].T)
        kr = k_rem[:, sl]
        upds.append([_mm(vt, prep(jnp.where(row_chunk == c, kr, 0.0))) for c in range(n_chunks)])
    befores = []
    for p, sl in enumerate(pairs):
        gl = glast[:, sl]
        sts = []
        st = get_state(p, 0) if chained else None
        for c in range(n_chunks):
            if not chained:
                st = get_state(p, c)
            sts.append(prep(st))
            decay = jnp.exp(gl[c * chunk:c * chunk + 1, :])
            st = st * decay + jnp.where(same_head, upds[p][c], 0.0)
            if not chained:
                put_state(p, c, st)
        if chained:
            put_state(p, n_chunks - 1, st)
        befores.append(sts)
    outs = []
    for p, sl in enumerate(pairs):
        qb = prep(q_dec[:, sl])
        inter = [_mm(tuple(t[c * chunk:(c + 1) * chunk] for t in qb), befores[p][c], nt=True)
                 for c in range(n_chunks)]
        outs.append(intra[p] + jnp.concatenate(inter, axis=0))
    return jnp.concatenate(outs, axis=1)


def _hgrn_finish(o, gate, bd, hgw):
    y = o * lax.rsqrt(_head_mean_sq(o, bd) + RMS_EPS) * hgw
    return y * (gate * jax.nn.sigmoid(gate))


def _hgrn_prompt_kernel(hq_ref, hf_ref, hi_ref, hg_ref, lb_ref, hgw_ref, ltri_ref, llast_ref, cm_ref, bd_ref,
                        y_ref, st_ref):
    @pl.when(pl.program_id(1) == 0)
    def _():
        st_ref[...] = jnp.zeros_like(st_ref)

    hi = hi_ref[0]
    q_dec, k_dec, k_rem, glast = _hgrn_rows(hq_ref[0], hf_ref[0], hi, lb_ref[...], ltri_ref[...], llast_ref[...])
    cmask = cm_ref[...] > 0

    def get_state(p, c):
        return st_ref[0, p]

    def put_state(p, c, val):
        st_ref[0, p] = val

    o = _hgrn_block(q_dec, k_dec, k_rem, glast, hi, cmask, HG_CHUNK, True, get_state, put_state)
    y_ref[0] = _hgrn_finish(o, hg_ref[0], bd_ref[...], hgw_ref[...]).astype(BF16)


def _hgrn_prompt(hq, hf, hi, hg, lb, hgw, masks, bd):
    b, t, _ = hq.shape
    rows = min(HG_ROWS, t)
    ltri, llast, cm = masks
    blk = pl.BlockSpec((1, rows, WIDTH), lambda bi, j: (bi, j, 0))
    full = lambda a: pl.BlockSpec(a.shape, lambda bi, j: (0,) * a.ndim)
    return pl.pallas_call(
        _hgrn_prompt_kernel,
        grid=(b, t // rows),
        in_specs=[blk, blk, blk, blk, full(lb), full(hgw), full(ltri), full(llast), full(cm), full(bd)],
        out_specs=[blk, pl.BlockSpec((1, N_PAIRS, LANES, LANES), lambda bi, j: (bi, 0, 0, 0))],
        out_shape=[jax.ShapeDtypeStruct((b, t, WIDTH), BF16),
                   jax.ShapeDtypeStruct((b, N_PAIRS, LANES, LANES), F32)],
        compiler_params=_cparams(("parallel", "arbitrary")),
        name="hgrn_prompt",
    )(hq, hf, hi, hg, lb, hgw, ltri, llast, cm, bd)


def _hgrn_decode_kernel(hq_ref, hf_ref, hi_ref, hg_ref, s0_ref, lb_ref, hgw_ref, ltri_ref, llast_ref, cm_ref, bd_ref,
                        y_ref, st_ref):
    hi = hi_ref[...]
    q_dec, k_dec, k_rem, glast = _hgrn_rows(hq_ref[...], hf_ref[...], hi, lb_ref[...], ltri_ref[...], llast_ref[...])
    cmask = cm_ref[...] > 0

    def get_state(p, c):
        return s0_ref[c, p]

    def put_state(p, c, val):
        st_ref[c, p] = val

    o = _hgrn_block(q_dec, k_dec, k_rem, glast, hi, cmask, DEC_CHUNK, False, get_state, put_state, precise=True)
    y_ref[...] = _hgrn_finish(o, hg_ref[...], bd_ref[...], hgw_ref[...])


def _hgrn_decode(hq, hf, hi, hg, s0, lb, hgw, masks, bd):
    n = hq.shape[0]
    ltri, llast, cm = masks
    full = lambda a: pl.BlockSpec(a.shape, lambda i: (0,) * a.ndim)
    args = (hq, hf, hi, hg, s0, lb, hgw, ltri, llast, cm, bd)
    return pl.pallas_call(
        _hgrn_decode_kernel,
        grid=(1,),
        in_specs=[full(a) for a in args],
        out_specs=[pl.BlockSpec((n, WIDTH), lambda i: (0, 0)), full(s0)],
        out_shape=[jax.ShapeDtypeStruct((n, WIDTH), F32), jax.ShapeDtypeStruct(s0.shape, F32)],
        compiler_params=_cparams(("arbitrary",)),
        name="hgrn_decode",
    )(*args)


def _chunk_masks(rows, chunk, valid):
    r = np.arange(rows)
    same = (r[:, None] // chunk) == (r[None, :] // chunk)
    ltri = same & (r[None, :] <= r[:, None])
    llast = same & ((r[None, :] % chunk) < valid)
    as_bf16 = lambda m: jnp.asarray(m.astype(np.float32), BF16)
    return as_bf16(ltri), as_bf16(llast), as_bf16(ltri)


def _mixout_kernel(x_ref, osb_ref, yb_ref, ga_ref, gb_ref, wa_ref, wb_ref, wo_ref, n2_ref, wr_ref, br_ref,
                   x1_ref, eid_ref, ew_ref, *, precise):
    if precise:
        prep = lambda x: _split_bf16(x)
    else:
        prep = lambda x: (x.astype(BF16),)
    ya = _mm(prep(osb_ref[...]), prep(wa_ref[...]))
    yb = _mm(prep(yb_ref[...]), prep(wb_ref[...]))
    merged = jax.nn.sigmoid(ga_ref[...]) * ya + jax.nn.sigmoid(gb_ref[...]) * yb
    x1 = x_ref[...] + _mm(prep(merged), prep(wo_ref[...]))
    x1_ref[...] = x1
    h2 = _rms_rows(x1, n2_ref[...])
    lg = _mm(prep(wr_ref[...]), prep(h2), nt=True) + br_ref[...]
    tm = lg.shape[1]
    gl = lg[0:8]
    gmax = jnp.max(gl, axis=0, keepdims=True)
    p_group = 1.0 / jnp.sum(jnp.exp(gl - gmax), axis=0, keepdims=True)
    gr = lax.broadcasted_iota(I32, (8, tm), 0)
    gidx = jnp.min(jnp.where(gl == gmax, gr, 8), axis=0, keepdims=True)
    el = lg[16:16 + N_EXPERTS]
    er = lax.broadcasted_iota(I32, (N_EXPERTS, tm), 0)
    e1 = jnp.where((er >> 3) == gidx, el, NEG)
    m1 = jnp.max(e1, axis=0, keepdims=True)
    i1 = jnp.min(jnp.where(e1 == m1, er, N_EXPERTS), axis=0, keepdims=True)
    e2 = jnp.where(er == i1, NEG, e1)
    m2 = jnp.max(e2, axis=0, keepdims=True)
    i2 = jnp.min(jnp.where(e2 == m2, er, N_EXPERTS), axis=0, keepdims=True)
    t = jnp.exp(m2 - m1)
    w1 = p_group / (1.0 + t)
    eid_ref[...] = jnp.concatenate([i1, i2], axis=0)
    ew_ref[...] = jnp.concatenate([w1, w1 * t], axis=0)


def _mix_out(x2d, osb, yb, ga, gb, wa, wb, wo, n2, wr, br, precise=False):
    n = x2d.shape[0]
    tm = min(ROW_TILE, n)
    row = lambda w: pl.BlockSpec((tm, w), lambda i: (i, 0))
    full = lambda a: pl.BlockSpec(a.shape, lambda i: (0,) * a.ndim)
    return pl.pallas_call(
        functools.partial(_mixout_kernel, precise=precise),
        grid=(n // tm,),
        in_specs=[row(D_MODEL), row(WIDTH), row(WIDTH), row(D_MODEL), row(D_MODEL),
                  full(wa), full(wb), full(wo), full(n2), full(wr), full(br)],
        out_specs=[row(D_MODEL), pl.BlockSpec((2, tm), lambda i: (0, i)), pl.BlockSpec((2, tm), lambda i: (0, i))],
        out_shape=[jax.ShapeDtypeStruct((n, D_MODEL), F32), jax.ShapeDtypeStruct((2, n), I32),
                   jax.ShapeDtypeStruct((2, n), F32)],
        compiler_params=_cparams(("parallel",)),
        name="mix_out",
    )(x2d, osb, yb, ga, gb, wa, wb, wo, n2, wr, br)


def _expert_kernel(texp_ref, nused_ref, tok_ref, tokn_ref, dst_ref,
                   x1_hbm, n2_ref, roww_ref, wg_ref, wu_ref, wd_ref, out_hbm,
                   xbuf, ybuf, gsem, ssem):
    i = pl.program_id(0)
    nt = pl.num_programs(0)
    nused = nused_ref[0]
    slot = lax.rem(i, 2)
    tm = ybuf.shape[0]

    def gather_copy(idx_ref, r, sl):
        return pltpu.make_async_copy(x1_hbm.at[pl.ds(idx_ref[0, 0, r], 1)], xbuf.at[sl, pl.ds(r, 1)], gsem.at[sl])

    def scatter_copy(r):
        return pltpu.make_async_copy(ybuf.at[pl.ds(r, 1)], out_hbm.at[pl.ds(dst_ref[0, 0, r], 1)], ssem.at[0])

    def for_rows(fn):
        def body(r, _):
            fn(r)
            return 0
        lax.fori_loop(0, tm, body, 0, unroll=8)

    @pl.when(i == 0)
    def _():
        ybuf[...] = jnp.zeros_like(ybuf)
        clear = pltpu.make_async_copy(ybuf, out_hbm.at[pl.ds(out_hbm.shape[0] - tm, tm)], ssem.at[0])
        clear.start()
        clear.wait()
        for_rows(lambda r: gather_copy(tok_ref, r, 0).start())

    @pl.when(i + 1 < nused)
    def _():
        for_rows(lambda r: gather_copy(tokn_ref, r, 1 - slot).start())

    @pl.when(i < nused)
    def _():
        for_rows(lambda r: gather_copy(tok_ref, r, slot).wait())
        h = _rms_rows(xbuf[slot], n2_ref[...]).astype(BF16)
        a = _dot(h, wg_ref[0])
        u = _dot(h, wu_ref[0])
        hb = (a * jax.nn.sigmoid(a) * u).astype(BF16)
        y = _dot(hb, wd_ref[0]) * roww_ref[...]

        @pl.when(i >= 1)
        def _():
            for_rows(lambda r: scatter_copy(r).wait())

        ybuf[...] = y
        for_rows(lambda r: scatter_copy(r).start())

        @pl.when(i == nt - 1)
        def _():
            for_rows(lambda r: scatter_copy(r).wait())

    @pl.when(i == nused)
    def _():
        for_rows(lambda r: scatter_copy(r).wait())


def _experts(x1, n2, meta, wg, wu, wd, tm):
    n = x1.shape[0]
    texp, nused, row_tok, row_dst, row_w = meta
    nt = texp.shape[0]
    tok3 = row_tok.reshape(nt, 1, tm)
    dst3 = row_dst.reshape(nt, 1, tm)
    smem_blk = lambda f: pl.BlockSpec((1, 1, tm), f, memory_space=pltpu.SMEM)
    return pl.pallas_call(
        _expert_kernel,
        grid_spec=pltpu.PrefetchScalarGridSpec(
            num_scalar_prefetch=2,
            grid=(nt,),
            in_specs=[smem_blk(lambda i, te, nu: (i, 0, 0)),
                      smem_blk(lambda i, te, nu: (jnp.minimum(i + 1, nt - 1), 0, 0)),
                      smem_blk(lambda i, te, nu: (i, 0, 0)),
                      pl.BlockSpec(memory_space=pl.ANY),
                      pl.BlockSpec((1, D_MODEL), lambda i, te, nu: (0, 0)),
                      pl.BlockSpec((tm, 1), lambda i, te, nu: (i, 0)),
                      pl.BlockSpec((1, D_MODEL, D_FF), lambda i, te, nu: (te[i], 0, 0)),
                      pl.BlockSpec((1, D_MODEL, D_FF), lambda i, te, nu: (te[i], 0, 0)),
                      pl.BlockSpec((1, D_FF, D_MODEL), lambda i, te, nu: (te[i], 0, 0))],
            out_specs=pl.BlockSpec(memory_space=pl.ANY),
            scratch_shapes=[pltpu.VMEM((2, tm, D_MODEL), F32),
                            pltpu.VMEM((tm, D_MODEL), F32),
                            pltpu.SemaphoreType.DMA((2,)),
                            pltpu.SemaphoreType.DMA((1,))],
        ),
        out_shape=jax.ShapeDtypeStruct((2 * n + tm, D_MODEL), F32),
        compiler_params=_cparams(("arbitrary",)),
        name="experts",
    )(texp, nused, tok3, tok3, dst3, x1, n2, row_w, wg, wu, wd)


def _moe_metadata(eid, ew, n_tok, tm):
    n_assign = 2 * n_tok
    e_flat = eid.reshape(-1)
    order = jnp.argsort(e_flat).astype(I32)
    counts = jnp.sum((e_flat[:, None] == jnp.arange(N_EXPERTS, dtype=I32)[None, :]).astype(I32), axis=0)
    starts = jnp.cumsum(counts) - counts
    padded = ((counts + tm - 1) // tm) * tm
    pad_ends = jnp.cumsum(padded)
    pad_starts = pad_ends - padded
    n_tiles = n_assign // tm + N_EXPERTS
    tile_lo = jnp.arange(n_tiles, dtype=I32) * tm
    texp = jnp.minimum(jnp.sum((pad_ends[None, :] <= tile_lo[:, None]).astype(I32), axis=1), N_EXPERTS - 1)
    done = tile_lo - pad_starts[texp]
    tval = jnp.clip(counts[texp] - done, 0, tm)
    nused = (pad_ends[-1:] // tm).astype(I32)
    within = jnp.arange(tm, dtype=I32)[None, :]
    valid = (within < tval[:, None]).reshape(-1)
    src = jnp.clip((starts[texp] + done)[:, None] + within, 0, n_assign - 1).reshape(-1)
    row_asg = jnp.where(valid, order[src], 0)
    row_tok = jnp.where(row_asg >= n_tok, row_asg - n_tok, row_asg)
    row_dst = jnp.where(valid, row_asg, n_assign + jnp.broadcast_to(within, (n_tiles, tm)).reshape(-1))
    row_w = jnp.where(valid, ew.reshape(-1)[row_asg], 0.0).reshape(-1, 1)
    return texp, nused, row_tok, row_dst, row_w


def _combine_kernel(x_ref, a_ref, b_ref, o_ref):
    o_ref[...] = x_ref[...] + (a_ref[...] + b_ref[...])


def _combine(x1, buf):
    n = x1.shape[0]
    tm = min(2 * ROW_TILE, n)
    nb = n // tm
    return pl.pallas_call(
        _combine_kernel,
        grid=(nb,),
        in_specs=[pl.BlockSpec((tm, D_MODEL), lambda i: (i, 0)),
                  pl.BlockSpec((tm, D_MODEL), lambda i: (i, 0)),
                  pl.BlockSpec((tm, D_MODEL), lambda i: (i + nb, 0))],
        out_specs=pl.BlockSpec((tm, D_MODEL), lambda i: (i, 0)),
        out_shape=jax.ShapeDtypeStruct((n, D_MODEL), F32),
        compiler_params=_cparams(("parallel",)),
        name="combine",
    )(x1, buf, buf)


def _moe(x1, n2, eid, ew, wg, wu, wd):
    n = x1.shape[0]
    tm = MOE_TILE if 2 * n >= N_EXPERTS * MOE_TILE else MOE_TILE_SMALL
    buf = _experts(x1, n2, _moe_metadata(eid, ew, n, tm), wg, wu, wd, tm)
    return _combine(x1, buf)


def _state_to_pairs(s):
    b = s.shape[0]
    st = jnp.swapaxes(s, -1, -2).reshape(b, N_PAIRS, 2, HEAD_DIM, HEAD_DIM)
    z = jnp.zeros_like(st[:, :, 0])
    top = jnp.concatenate([st[:, :, 0], z], axis=-1)
    bot = jnp.concatenate([z, st[:, :, 1]], axis=-1)
    return jnp.concatenate([top, bot], axis=-2)


def _pairs_to_state(st):
    b = st.shape[0]
    h0 = st[:, :, :HEAD_DIM, :HEAD_DIM]
    h1 = st[:, :, HEAD_DIM:, HEAD_DIM:]
    s = jnp.stack([h0, h1], axis=2).reshape(b, N_HEADS, HEAD_DIM, HEAD_DIM)
    return jnp.swapaxes(s, -1, -2)


def kernel(x_prompt, x_sample, cache_k, cache_v, page_table, state_hgrn, norm1_w, w_in, q_norm_w, k_norm_w, sb_bias,
           lb_logits, hg_norm_w, w_branch_a, w_branch_b, w_out, norm2_w, w_group_router, b_group_router,
           w_expert_router, b_expert_router, w_gate_exp, w_up_exp, w_down_exp):
    depth = w_in.shape[0]
    assert depth == 1, "single layer"
    l = 0
    b, t, _ = x_prompt.shape
    bd_, tq, _ = x_sample.shape
    n_p = b * t
    n_s = bd_ * tq
    assert t % HG_CHUNK == 0 and tq <= DEC_CHUNK and tq % HG_CHUNK != 0

    lbs = jnp.cumsum(jax.nn.softmax(lb_logits.astype(F32), axis=0), axis=0)
    lb = lbs[l].reshape(1, WIDTH)
    n1 = norm1_w[l].reshape(1, D_MODEL)
    n2 = norm2_w[l].reshape(1, D_MODEL)
    w_in_b = w_in[l].astype(BF16)
    qw = jnp.tile(q_norm_w[l], N_HEADS).reshape(1, WIDTH)
    kw = jnp.tile(k_norm_w[l], N_HEADS).reshape(1, WIDTH)
    hgw = jnp.tile(hg_norm_w[l], N_HEADS).reshape(1, WIDTH)
    hd = np.arange(WIDTH) // HEAD_DIM
    bdiag = jnp.asarray((hd[:, None] == hd[None, :]).astype(np.float32), BF16)
    bias2 = sb_bias[l].astype(F32) * LOG2E
    wa = w_branch_a[l].astype(BF16)
    wb = w_branch_b[l].astype(BF16)
    wo = w_out[l].astype(BF16)
    wr32 = jnp.zeros((ROUTER_ROWS, D_MODEL), F32)
    wr32 = wr32.at[0:N_GROUPS].set(w_group_router[l].T).at[16:16 + N_EXPERTS].set(w_expert_router[l].T)
    wr = wr32.astype(BF16)
    br = jnp.full((ROUTER_ROWS, 1), NEG, F32)
    br = br.at[0:N_GROUPS, 0].set(b_group_router[l]).at[16:16 + N_EXPERTS, 0].set(b_expert_router[l])
    wg = w_gate_exp[l].astype(BF16)
    wu = w_up_exp[l].astype(BF16)
    wd = w_down_exp[l].astype(BF16)

    def tri2_ge(n):
        r = np.arange(n)
        tri = (r[:, None] >= r[None, :]).astype(np.float32)
        return jnp.asarray(np.concatenate([tri, tri], axis=0), BF16)

    xp = x_prompt.reshape(n_p, D_MODEL)
    q, k, kb, v, vb, hq, hf, hi, hg, ga, gb = _in_proj(xp, n1, w_in_b, bdiag, qw, kw, t_len=t)
    o_sb = _sb_prompt(q.reshape(b, t, WIDTH), kb.reshape(b, t, WIDTH), vb.reshape(b, t, WIDTH), bias2,
                      tri2_ge(min(SB_TILE, t)))
    r3 = lambda a: a.reshape(b, t, WIDTH)
    y_b, st_p = _hgrn_prompt(r3(hq), r3(hf), r3(hi), r3(hg), lb, hgw,
                             _chunk_masks(min(HG_ROWS, t), HG_CHUNK, HG_CHUNK), bdiag)
    x1, eid, ew = _mix_out(xp, o_sb.reshape(n_p, WIDTH), y_b.reshape(n_p, WIDTH), ga, gb, wa, wb, wo, n2, wr, br)
    y_prompt = _moe(x1, n2, eid, ew, wg, wu, wd).reshape(b, t, D_MODEL)
    to_kv = lambda a: jnp.transpose(a.reshape(1, b, N_HEADS, HEAD_DIM, t), (0, 1, 4, 2, 3))
    new_k_prompt = to_kv(k)
    new_v_prompt = to_kv(v)
    new_hgrn_prompt = _pairs_to_state(st_p)[None]

    xs = x_sample.reshape(n_s, D_MODEL)
    q, k, v, hq, hf, hi, hg, ga, gb = _in_proj_precise(xs, n1, w_in[l], bdiag, qw, kw)
    pad_keys = lambda a: jnp.pad(jnp.swapaxes(a.reshape(bd_, tq, WIDTH), 1, 2), ((0, 0), (0, 0), (0, PAGE - tq)))
    n_pool = cache_k.shape[1]
    pages_t = lambda c: jnp.transpose(c, (0, 2, 3, 1)).reshape(n_pool, WIDTH, PAGE)
    o_sb = _sb_decode(q.reshape(bd_, tq, WIDTH), pad_keys(k), pad_keys(v), pages_t(cache_k[l]), pages_t(cache_v[l]),
                      page_table, bias2, tri2_ge(PAGE))
    pad_rows = lambda a: jnp.pad(a.reshape(bd_, tq, WIDTH), ((0, 0), (0, DEC_CHUNK - tq), (0, 0))).reshape(
        bd_ * DEC_CHUNK, WIDTH)
    y_b, st_s = _hgrn_decode(pad_rows(hq), pad_rows(hf), pad_rows(hi), pad_rows(hg),
                             _state_to_pairs(state_hgrn[l].astype(F32)), lb, hgw,
                             _chunk_masks(bd_ * DEC_CHUNK, DEC_CHUNK, tq), bdiag)
    y_b = y_b.reshape(bd_, DEC_CHUNK, WIDTH)[:, :tq].reshape(n_s, WIDTH)
    x1, eid, ew = _mix_out(xs, o_sb.reshape(n_s, WIDTH), y_b, ga, gb, w_branch_a[l], w_branch_b[l], w_out[l], n2,
                           wr32, br, precise=True)
    y_sample = _moe(x1, n2, eid, ew, wg, wu, wd).reshape(bd_, tq, D_MODEL)
    new_k_sample = k.reshape(1, bd_, tq, N_HEADS, HEAD_DIM)
    new_v_sample = v.reshape(1, bd_, tq, N_HEADS, HEAD_DIM)
    new_hgrn_sample = _pairs_to_state(st_s)[None]

    return (y_prompt, y_sample, new_k_prompt, new_v_prompt, new_hgrn_prompt,
            new_k_sample, new_v_sample, new_hgrn_sample)
```

```python
import functools

import jax
import jax.numpy as jnp
import numpy as np
from jax import lax
from jax.experimental import pallas as pl
from jax.experimental.pallas import tpu as pltpu

F32 = jnp.float32
BF16 = jnp.bfloat16
I32 = jnp.int32

D_MODEL = 1024
N_HEADS = 8
HEAD_DIM = 64
WIDTH = N_HEADS * HEAD_DIM
LANES = 128
N_PAIRS = WIDTH // LANES
N_GROUPS = 4
EXPERTS_PER_GROUP = 8
N_EXPERTS = N_GROUPS * EXPERTS_PER_GROUP
D_FF = 512
PAGE = 128
HG_CHUNK = 32
RMS_EPS = 1e-6
NEG = -1e30
LOG2E = 1.4426950408889634

VMEM_LIMIT = 56 * 1024 * 1024
ROW_TILE = 256
SB_TILE = 256
SB_PAIRS = 4
DEC_GROUP = 8
HG_ROWS = 256
MOE_TILE = 256
MOE_TILE_SMALL = 32
PAGES_PER_STEP = 32
DEC_CHUNK = 16
ROUTER_ROWS = 48


def _cparams(sem):
    return pltpu.CompilerParams(dimension_semantics=sem, vmem_limit_bytes=VMEM_LIMIT)


def _dot(a, b):
    return jnp.dot(a, b, preferred_element_type=F32)


def _dot_nt(a, b):
    return lax.dot_general(a, b, (((1,), (1,)), ((), ())), preferred_element_type=F32)


def _split_bf16(x):
    hi = x.astype(BF16)
    lo = (x - hi.astype(F32)).astype(BF16)
    return hi, lo


def _prep(x, precise):
    return _split_bf16(x) if precise else (x.astype(BF16),)


def _mm(pa, pb, nt=False):
    d = _dot_nt if nt else _dot
    out = d(pa[0], pb[0])
    if len(pa) == 2:
        out = out + (d(pa[0], pb[1]) + d(pa[1], pb[0]))
    return out


def _dot_split(x, m):
    hi, lo = _split_bf16(x)
    return _dot(hi, m) + _dot(lo, m)


def _head_mean_sq(h, bd):
    return _dot_split(h * h, bd) * (1.0 / HEAD_DIM)


def _rms_rows(x, w):
    ms = jnp.mean(x * x, axis=-1, keepdims=True)
    return x * lax.rsqrt(ms + RMS_EPS) * w


def _softplus2(u):
    return jnp.maximum(u, 0.0) + jnp.log2(1.0 + jnp.exp2(-jnp.abs(u)))


def _cum_ge(sp, tri2):
    hi, lo = _split_bf16(sp)
    return _dot(jnp.concatenate([hi, lo], axis=1), tri2)


_SEGS = {"q": (0, 512), "k": (512, 1024), "v": (1024, 1536), "hq": (1536, 2048), "hf": (2048, 2560),
         "hi": (2560, 3072), "hg": (3072, 3584), "ga": (3584, 4608), "gb": (4608, 5632)}
D_IN = 5632


def _inproj_kernel(x_ref, n1_ref, w_ref, bd_ref, qw_ref, kw_ref,
                   q_ref, k_ref, kb_ref, v_ref, vb_ref, hq_ref, hf_ref, hi_ref, hg_ref, ga_ref, gb_ref, *, kv_t):
    xn = _rms_rows(x_ref[...], n1_ref[...]).astype(BF16)
    bd = bd_ref[...]

    def seg(name):
        a, b = _SEGS[name]
        return _dot(xn, w_ref[:, a:b])

    hq = seg("q")
    hk = seg("k")
    hv = seg("v")
    hq_ref[...] = seg("hq")
    hf_ref[...] = seg("hf")
    hi_ref[...] = seg("hi")
    hg_ref[...] = seg("hg")
    ga_ref[...] = seg("ga")
    gb_ref[...] = seg("gb")
    qn = hq * lax.rsqrt(_head_mean_sq(hq, bd) + RMS_EPS) * qw_ref[...]
    q_ref[...] = (qn * (HEAD_DIM ** -0.5 * LOG2E)).astype(BF16)
    kn = hk * lax.rsqrt(_head_mean_sq(hk, bd) + RMS_EPS) * kw_ref[...]
    kb_ref[...] = kn.astype(BF16)
    vb_ref[...] = hv.astype(BF16)
    if kv_t:
        k_ref[0] = kn.T
        v_ref[0] = hv.T
    else:
        k_ref[...] = kn
        v_ref[...] = hv


def _in_proj(x2d, n1, w_in, bd, qw, kw, t_len=None):
    n = x2d.shape[0]
    tm = min(ROW_TILE, n)
    row = lambda w: pl.BlockSpec((tm, w), lambda i: (i, 0))
    full = lambda a: pl.BlockSpec(a.shape, lambda i: (0,) * a.ndim)
    if t_len is None:
        kv_spec = row(WIDTH)
        kv_shape = jax.ShapeDtypeStruct((n, WIDTH), F32)
    else:
        per_seq = t_len // tm
        kv_spec = pl.BlockSpec((1, WIDTH, tm), lambda i: (i // per_seq, 0, i % per_seq))
        kv_shape = jax.ShapeDtypeStruct((n // t_len, WIDTH, t_len), F32)
    rows = lambda w, dt: (row(w), jax.ShapeDtypeStruct((n, w), dt))
    outs = [rows(WIDTH, BF16), (kv_spec, kv_shape), rows(WIDTH, BF16), (kv_spec, kv_shape), rows(WIDTH, BF16),
            rows(WIDTH, F32), rows(WIDTH, F32), rows(WIDTH, F32), rows(WIDTH, F32),
            rows(D_MODEL, F32), rows(D_MODEL, F32)]
    return pl.pallas_call(
        functools.partial(_inproj_kernel, kv_t=t_len is not None),
        grid=(n // tm,),
        in_specs=[row(D_MODEL), full(n1), full(w_in), full(bd), full(qw), full(kw)],
        out_specs=[s for s, _ in outs],
        out_shape=[s for _, s in outs],
        compiler_params=_cparams(("parallel",)),
        name="in_proj",
    )(x2d, n1, w_in, bd, qw, kw)


def _inproj_precise_kernel(x_ref, n1_ref, w_ref, bd_ref, qw_ref, kw_ref, h_ref):
    j = pl.program_id(0)
    xn = _rms_rows(x_ref[...], n1_ref[...])
    h = _mm(_split_bf16(xn), _split_bf16(w_ref[...]))
    bd = bd_ref[...]

    @pl.when(j == 0)
    def _():
        h_ref[...] = h * lax.rsqrt(_head_mean_sq(h, bd) + RMS_EPS) * qw_ref[...] * (HEAD_DIM ** -0.5 * LOG2E)

    @pl.when(j == 1)
    def _():
        h_ref[...] = h * lax.rsqrt(_head_mean_sq(h, bd) + RMS_EPS) * kw_ref[...]

    @pl.when(j >= 2)
    def _():
        h_ref[...] = h


def _in_proj_precise(x2d, n1, w_in, bd, qw, kw):
    n = x2d.shape[0]
    full = lambda a: pl.BlockSpec(a.shape, lambda j: (0,) * a.ndim)
    h = pl.pallas_call(
        _inproj_precise_kernel,
        grid=(D_IN // WIDTH,),
        in_specs=[full(x2d), full(n1), pl.BlockSpec((D_MODEL, WIDTH), lambda j: (0, j)), full(bd), full(qw), full(kw)],
        out_specs=pl.BlockSpec((n, WIDTH), lambda j: (0, j)),
        out_shape=jax.ShapeDtypeStruct((n, D_IN), F32),
        compiler_params=_cparams(("arbitrary",)),
        name="in_proj_precise",
    )(x2d, n1, w_in, bd, qw, kw)
    return [h[:, a:b] for a, b in _SEGS.values()]


def _sb_prompt_kernel(bias_ref, q_ref, k_ref, v_ref, tri2_ref, o_ref, *, tile):
    p = pl.program_id(1)
    i = pl.program_id(2)
    nh = 2 * SB_PAIRS
    lane = lax.broadcasted_iota(I32, (tile, LANES), 1)
    first = lane < HEAD_DIM
    qs = []
    for pp in range(SB_PAIRS):
        q = q_ref[0, :, pp * LANES:(pp + 1) * LANES]
        zero = jnp.zeros_like(q)
        qs += [jnp.where(first, q, zero), jnp.where(first, zero, q)]
    bias = [bias_ref[nh * p + h] for h in range(nh)]
    tri2 = tri2_ref[...]
    row = lax.broadcasted_iota(I32, (tile, tile), 0)
    col = lax.broadcasted_iota(I32, (tile, tile), 1)
    causal = col < row

    def step(j, carry, accs, diag):
        start = pl.multiple_of(j * tile, tile)
        pair = lambda h: slice((h // 2) * LANES, (h // 2 + 1) * LANES)
        us = [_dot_nt(qs[h], k_ref[0, pl.ds(start, tile), pair(h)]) + bias[h] for h in range(nh)]
        cums = []
        for h in range(nh):
            sp = _softplus2(us[h])
            if diag:
                sp = jnp.where(causal, sp, 0.0)
            cums.append(_cum_ge(sp, tri2) + carry[h])
        new_a = []
        for h in range(nh):
            a = jnp.exp2(us[h] - cums[h])
            if diag:
                a = jnp.where(causal, a, 0.0)
            new_a.append(accs[h] + _dot(a.astype(BF16), v_ref[0, pl.ds(start, tile), pair(h)]))
        return tuple(c[:, 0:1] for c in cums), tuple(new_a)

    zc = tuple(jnp.zeros((tile, 1), F32) for _ in range(nh))
    za = tuple(jnp.zeros((tile, LANES), F32) for _ in range(nh))
    carry, accs = step(i, zc, za, True)

    def body(n, state):
        return step(i - 1 - n, state[0], state[1], False)

    carry, accs = lax.fori_loop(0, i, body, (carry, accs))
    for pp in range(SB_PAIRS):
        o_ref[0, :, pp * LANES:(pp + 1) * LANES] = jnp.where(first, accs[2 * pp], accs[2 * pp + 1]).astype(BF16)


def _sb_prompt(q, k, v, bias2, tri2):
    b, t, _ = q.shape
    tile = tri2.shape[1]
    w = SB_PAIRS * LANES
    kern = functools.partial(_sb_prompt_kernel, tile=tile)
    return pl.pallas_call(
        kern,
        grid_spec=pltpu.PrefetchScalarGridSpec(
            num_scalar_prefetch=0,
            grid=(b, WIDTH // w, t // tile),
            in_specs=[pl.BlockSpec(memory_space=pltpu.SMEM),
                      pl.BlockSpec((1, tile, w), lambda bi, p, i: (bi, i, p)),
                      pl.BlockSpec((1, t, w), lambda bi, p, i: (bi, 0, p)),
                      pl.BlockSpec((1, t, w), lambda bi, p, i: (bi, 0, p)),
                      pl.BlockSpec((2 * tile, tile), lambda bi, p, i: (0, 0))],
            out_specs=pl.BlockSpec((1, tile, w), lambda bi, p, i: (bi, i, p)),
        ),
        out_shape=jax.ShapeDtypeStruct((b, t, WIDTH), BF16),
        compiler_params=_cparams(("parallel", "parallel", "arbitrary")),
        name="sb_prompt",
    )(bias2, q, k, v, tri2)


def _sb_decode_kernel(pt_ref, bias_ref, q_ref, knt_ref, vnt_ref, ck_hbm, cv_hbm, tri2_ref, o_ref,
                      kbuf, vbuf, sem, qbd_s, bias_s, acc_s, carry_s, *, n_pages, n_q):
    tri2 = tri2_ref[...]
    b = pl.program_id(0)
    s = pl.program_id(1)
    nb = pl.num_programs(0)
    ns = pl.num_programs(1)
    pp = kbuf.shape[1]
    g = b * ns + s
    slot = lax.rem(g, 2)
    rows = n_q * N_HEADS

    def page_copies(bb, ss, sl, i):
        page = pt_ref[bb, n_pages - (ss + 1) * pp + i]
        return (pltpu.make_async_copy(ck_hbm.at[page], kbuf.at[sl, i], sem.at[0, sl]),
                pltpu.make_async_copy(cv_hbm.at[page], vbuf.at[sl, i], sem.at[1, sl]))

    def issue(bb, ss, sl):
        for i in range(pp):
            ck, cv = page_copies(bb, ss, sl, i)
            ck.start()
            cv.start()

    @pl.when(g == 0)
    def _():
        issue(0, 0, 0)

    @pl.when(g + 1 < nb * ns)
    def _():
        last = s + 1 == ns
        issue(jnp.where(last, b + 1, b), jnp.where(last, 0, s + 1), 1 - slot)

    head_of_lane = lax.broadcasted_iota(I32, (N_HEADS, WIDTH), 1) >> 6
    head_of_row = lax.broadcasted_iota(I32, (N_HEADS, WIDTH), 0)
    head_mask = head_of_lane == head_of_row

    def tall(a2, pb, nt):
        d = _dot_nt if nt else _dot
        both = d(a2, pb[0])
        out = both[:rows] + both[rows:]
        if len(pb) == 2:
            out = out + d(a2[:rows], pb[1])
        return out

    @pl.when(s == 0)
    def _():
        qf = q_ref[0]
        for qi in range(n_q):
            blk = jnp.where(head_mask, jnp.broadcast_to(qf[qi:qi + 1, :], (N_HEADS, WIDTH)), 0.0)
            hi, lo = _split_bf16(blk)
            qbd_s[qi * N_HEADS:(qi + 1) * N_HEADS, :] = hi
            qbd_s[rows + qi * N_HEADS:rows + (qi + 1) * N_HEADS, :] = lo
        r = lax.broadcasted_iota(I32, (rows, LANES), 0) & (N_HEADS - 1)
        bv = jnp.zeros((rows, LANES), F32)
        for h in range(N_HEADS):
            bv = jnp.where(r == h, bias_ref[h], bv)
        bias_s[...] = bv
        u = tall(qbd_s[...], _split_bf16(knt_ref[0]), False) + bv
        key = lax.broadcasted_iota(I32, (rows, LANES), 1)
        qidx = lax.broadcasted_iota(I32, (rows, LANES), 0) >> 3
        mask = key < qidx
        cum = _cum_ge(jnp.where(mask, _softplus2(u), 0.0), tri2)
        a = jnp.where(mask, jnp.exp2(u - cum), 0.0)
        acc_s[...] = tall(jnp.concatenate(_split_bf16(a), axis=0), _split_bf16(vnt_ref[0]), True)
        carry_s[...] = jnp.broadcast_to(cum[:, 0:1], (rows, LANES))

    for i in range(pp):
        ck, cv = page_copies(b, s, slot, i)
        ck.wait()
        cv.wait()

    qbd = qbd_s[...]
    bias = bias_s[...]
    group = min(DEC_GROUP, pp)

    def body(n, _):
        js = [(pp // group - 1 - n) * group + (group - 1 - t) for t in range(group)]
        us = [tall(qbd, (kbuf[slot, j].astype(BF16),), False) + bias for j in js]
        sums = [_cum_ge(_softplus2(u), tri2) for u in us]
        carry = carry_s[...]
        acc = acc_s[...]
        for t, j in enumerate(js):
            cum = sums[t] + carry
            a = jnp.exp2(us[t] - cum)
            acc = acc + tall(jnp.concatenate(_split_bf16(a), axis=0), (vbuf[slot, j].astype(BF16),), True)
            carry = jnp.broadcast_to(cum[:, 0:1], (rows, LANES))
        acc_s[...] = acc
        carry_s[...] = carry
        return 0

    lax.fori_loop(0, pp // group, body, 0)

    @pl.when(s == ns - 1)
    def _():
        acc = acc_s[...]
        for qi in range(n_q):
            blk = jnp.where(head_mask, acc[qi * N_HEADS:(qi + 1) * N_HEADS, :], 0.0)
            o_ref[0, qi:qi + 1, :] = jnp.sum(blk, axis=0, keepdims=True)


def _sb_decode(q, knt, vnt, ckt, cvt, page_table, bias2, tri2):
    bd, n_q, _ = q.shape
    n_pages = page_table.shape[1]
    pp = min(PAGES_PER_STEP, n_pages)
    rows = n_q * N_HEADS
    kern = functools.partial(_sb_decode_kernel, n_pages=n_pages, n_q=n_q)
    return pl.pallas_call(
        kern,
        grid_spec=pltpu.PrefetchScalarGridSpec(
            num_scalar_prefetch=1,
            grid=(bd, n_pages // pp),
            in_specs=[pl.BlockSpec(memory_space=pltpu.SMEM),
                      pl.BlockSpec((1, n_q, WIDTH), lambda b, s, pt: (b, 0, 0)),
                      pl.BlockSpec((1, WIDTH, PAGE), lambda b, s, pt: (b, 0, 0)),
                      pl.BlockSpec((1, WIDTH, PAGE), lambda b, s, pt: (b, 0, 0)),
                      pl.BlockSpec(memory_space=pl.ANY),
                      pl.BlockSpec(memory_space=pl.ANY),
                      pl.BlockSpec((2 * PAGE, PAGE), lambda b, s, pt: (0, 0))],
            out_specs=pl.BlockSpec((1, n_q, WIDTH), lambda b, s, pt: (b, 0, 0)),
            scratch_shapes=[pltpu.VMEM((2, pp, WIDTH, PAGE), F32),
                            pltpu.VMEM((2, pp, WIDTH, PAGE), F32),
                            pltpu.SemaphoreType.DMA((2, 2)),
                            pltpu.VMEM((2 * rows, WIDTH), BF16),
                            pltpu.VMEM((rows, LANES), F32),
                            pltpu.VMEM((rows, WIDTH), F32),
                            pltpu.VMEM((rows, LANES), F32)],
        ),
        out_shape=jax.ShapeDtypeStruct((bd, n_q, WIDTH), F32),
        compiler_params=_cparams(("arbitrary", "arbitrary")),
        name="sb_decode",
    )(page_table, bias2, q, knt, vnt, ckt, cvt, tri2)


def _hgrn_rows(hq, hf, hi, lb, ltri, llast):
    f = lb + (1.0 - lb) * jax.nn.sigmoid(hf)
    g = jnp.log(f)
    kk = 1.0 - f
    g_hi, g_lo = _split_bf16(g)
    gcum = _dot(ltri, g_hi) + _dot(ltri, g_lo)
    glast = _dot(llast, g_hi) + _dot(llast, g_lo)
    q_dec = hq * jnp.exp(gcum)
    k_dec = kk * jnp.exp(-gcum)
    k_rem = kk * jnp.exp(glast - gcum)
    return q_dec, k_dec, k_rem, glast


def _hgrn_block(q_dec, k_dec, k_rem, glast, hi, cmask, chunk, chained, get_state, put_state, precise=False):
    n_rows = q_dec.shape[0]
    n_chunks = n_rows // chunk
    pairs = [slice(p * LANES, (p + 1) * LANES) for p in range(N_PAIRS)]
    first = lax.broadcasted_iota(I32, (n_rows, LANES), 1) < HEAD_DIM
    row_chunk = lax.broadcasted_iota(I32, (n_rows, LANES), 0) >> (chunk.bit_length() - 1)
    sr = lax.broadcasted_iota(I32, (LANES, LANES), 0) >> 6
    sc = lax.broadcasted_iota(I32, (LANES, LANES), 1) >> 6
    same_head = sr == sc
    prep = lambda x: _prep(x, precise)
    vbs = [prep(hi[:, sl]) for sl in pairs]
    scores = []
    for sl in pairs:
        q = q_dec[:, sl]
        kb = prep(k_dec[:, sl])
        s0 = _mm(prep(jnp.where(first, q, 0.0)), kb, nt=True)
        s1 = _mm(prep(jnp.where(first, 0.0, q)), kb, nt=True)
        scores.append((prep(jnp.where(cmask, s0, 0.0)), prep(jnp.where(cmask, s1, 0.0))))
    intra = [jnp.where(first, _mm(s0, vb), _mm(s1, vb)) for (s0, s1), vb in zip(scores, vbs)]
    upds = []
    for sl in pairs:
        vt = prep(hi[:, sl].T)
        kr = k_rem[:, sl]
        upds.append([_mm(vt, prep(jnp.where(row_chunk == c, kr, 0.0))) for c in range(n_chunks)])
    befores = []
    for p, sl in enumerate(pairs):
        gl = glast[:, sl]
        sts = []
        st = get_state(p, 0) if chained else None
        for c in range(n_chunks):
            if not chained:
                st = get_state(p, c)
            sts.append(prep(st))
            decay = jnp.exp(gl[c * chunk:c * chunk + 1, :])
            st = st * decay + jnp.where(same_head, upds[p][c], 0.0)
            if not chained:
                put_state(p, c, st)
        if chained:
            put_state(p, n_chunks - 1, st)
        befores.append(sts)
    outs = []
    for p, sl in enumerate(pairs):
        qb = prep(q_dec[:, sl])
        inter = [_mm(tuple(t[c * chunk:(c + 1) * chunk] for t in qb), befores[p][c], nt=True)
                 for c in range(n_chunks)]
        outs.append(intra[p] + jnp.concatenate(inter, axis=0))
    return jnp.concatenate(outs, axis=1)


def _hgrn_finish(o, gate, bd, hgw):
    y = o * lax.rsqrt(_head_mean_sq(o, bd) + RMS_EPS) * hgw
    return y * (gate * jax.nn.sigmoid(gate))


def _hgrn_prompt_kernel(hq_ref, hf_ref, hi_ref, hg_ref, lb_ref, hgw_ref, ltri_ref, llast_ref, cm_ref, bd_ref,
                        y_ref, st_ref):
    @pl.when(pl.program_id(1) == 0)
    def _():
        st_ref[...] = jnp.zeros_like(st_ref)

    hi = hi_ref[0]
    q_dec, k_dec, k_rem, glast = _hgrn_rows(hq_ref[0], hf_ref[0], hi, lb_ref[...], ltri_ref[...], llast_ref[...])
    cmask = cm_ref[...] > 0

    def get_state(p, c):
        return st_ref[0, p]

    def put_state(p, c, val):
        st_ref[0, p] = val

    o = _hgrn_block(q_dec, k_dec, k_rem, glast, hi, cmask, HG_CHUNK, True, get_state, put_state)
    y_ref[0] = _hgrn_finish(o, hg_ref[0], bd_ref[...], hgw_ref[...]).astype(BF16)


def _hgrn_prompt(hq, hf, hi, hg, lb, hgw, masks, bd):
    b, t, _ = hq.shape
    rows = min(HG_ROWS, t)
    ltri, llast, cm = masks
    blk = pl.BlockSpec((1, rows, WIDTH), lambda bi, j: (bi, j, 0))
    full = lambda a: pl.BlockSpec(a.shape, lambda bi, j: (0,) * a.ndim)
    return pl.pallas_call(
        _hgrn_prompt_kernel,
        grid=(b, t // rows),
        in_specs=[blk, blk, blk, blk, full(lb), full(hgw), full(ltri), full(llast), full(cm), full(bd)],
        out_specs=[blk, pl.BlockSpec((1, N_PAIRS, LANES, LANES), lambda bi, j: (bi, 0, 0, 0))],
        out_shape=[jax.ShapeDtypeStruct((b, t, WIDTH), BF16),
                   jax.ShapeDtypeStruct((b, N_PAIRS, LANES, LANES), F32)],
        compiler_params=_cparams(("parallel", "arbitrary")),
        name="hgrn_prompt",
    )(hq, hf, hi, hg, lb, hgw, ltri, llast, cm, bd)


def _hgrn_decode_kernel(hq_ref, hf_ref, hi_ref, hg_ref, s0_ref, lb_ref, hgw_ref, ltri_ref, llast_ref, cm_ref, bd_ref,
                        y_ref, st_ref):
    hi = hi_ref[...]
    q_dec, k_dec, k_rem, glast = _hgrn_rows(hq_ref[...], hf_ref[...], hi, lb_ref[...], ltri_ref[...], llast_ref[...])
    cmask = cm_ref[...] > 0

    def get_state(p, c):
        return s0_ref[c, p]

    def put_state(p, c, val):
        st_ref[c, p] = val

    o = _hgrn_block(q_dec, k_dec, k_rem, glast, hi, cmask, DEC_CHUNK, False, get_state, put_state, precise=True)
    y_ref[...] = _hgrn_finish(o, hg_ref[...], bd_ref[...], hgw_ref[...])


def _hgrn_decode(hq, hf, hi, hg, s0, lb, hgw, masks, bd):
    n = hq.shape[0]
    ltri, llast, cm = masks
    full = lambda a: pl.BlockSpec(a.shape, lambda i: (0,) * a.ndim)
    args = (hq, hf, hi, hg, s0, lb, hgw, ltri, llast, cm, bd)
    return pl.pallas_call(
        _hgrn_decode_kernel,
        grid=(1,),
        in_specs=[full(a) for a in args],
        out_specs=[pl.BlockSpec((n, WIDTH), lambda i: (0, 0)), full(s0)],
        out_shape=[jax.ShapeDtypeStruct((n, WIDTH), F32), jax.ShapeDtypeStruct(s0.shape, F32)],
        compiler_params=_cparams(("arbitrary",)),
        name="hgrn_decode",
    )(*args)


def _chunk_masks(rows, chunk, valid):
    r = np.arange(rows)
    same = (r[:, None] // chunk) == (r[None, :] // chunk)
    ltri = same & (r[None, :] <= r[:, None])
    llast = same & ((r[None, :] % chunk) < valid)
    as_bf16 = lambda m: jnp.asarray(m.astype(np.float32), BF16)
    return as_bf16(ltri), as_bf16(llast), as_bf16(ltri)


def _mixout_kernel(x_ref, osb_ref, yb_ref, ga_ref, gb_ref, wa_ref, wb_ref, wo_ref, n2_ref, wr_ref, br_ref,
                   x1_ref, eid_ref, ew_ref, *, precise):
    if precise:
        prep = lambda x: _split_bf16(x)
    else:
        prep = lambda x: (x.astype(BF16),)
    ya = _mm(prep(osb_ref[...]), prep(wa_ref[...]))
    yb = _mm(prep(yb_ref[...]), prep(wb_ref[...]))
    merged = jax.nn.sigmoid(ga_ref[...]) * ya + jax.nn.sigmoid(gb_ref[...]) * yb
    x1 = x_ref[...] + _mm(prep(merged), prep(wo_ref[...]))
    x1_ref[...] = x1
    h2 = _rms_rows(x1, n2_ref[...])
    lg = _mm(prep(wr_ref[...]), prep(h2), nt=True) + br_ref[...]
    tm = lg.shape[1]
    gl = lg[0:8]
    gmax = jnp.max(gl, axis=0, keepdims=True)
    p_group = 1.0 / jnp.sum(jnp.exp(gl - gmax), axis=0, keepdims=True)
    gr = lax.broadcasted_iota(I32, (8, tm), 0)
    gidx = jnp.min(jnp.where(gl == gmax, gr, 8), axis=0, keepdims=True)
    el = lg[16:16 + N_EXPERTS]
    er = lax.broadcasted_iota(I32, (N_EXPERTS, tm), 0)
    e1 = jnp.where((er >> 3) == gidx, el, NEG)
    m1 = jnp.max(e1, axis=0, keepdims=True)
    i1 = jnp.min(jnp.where(e1 == m1, er, N_EXPERTS), axis=0, keepdims=True)
    e2 = jnp.where(er == i1, NEG, e1)
    m2 = jnp.max(e2, axis=0, keepdims=True)
    i2 = jnp.min(jnp.where(e2 == m2, er, N_EXPERTS), axis=0, keepdims=True)
    t = jnp.exp(m2 - m1)
    w1 = p_group / (1.0 + t)
    eid_ref[...] = jnp.concatenate([i1, i2], axis=0)
    ew_ref[...] = jnp.concatenate([w1, w1 * t], axis=0)


def _mix_out(x2d, osb, yb, ga, gb, wa, wb, wo, n2, wr, br, precise=False):
    n = x2d.shape[0]
    tm = min(ROW_TILE, n)
    row = lambda w: pl.BlockSpec((tm, w), lambda i: (i, 0))
    full = lambda a: pl.BlockSpec(a.shape, lambda i: (0,) * a.ndim)
    return pl.pallas_call(
        functools.partial(_mixout_kernel, precise=precise),
        grid=(n // tm,),
        in_specs=[row(D_MODEL), row(WIDTH), row(WIDTH), row(D_MODEL), row(D_MODEL),
                  full(wa), full(wb), full(wo), full(n2), full(wr), full(br)],
        out_specs=[row(D_MODEL), pl.BlockSpec((2, tm), lambda i: (0, i)), pl.BlockSpec((2, tm), lambda i: (0, i))],
        out_shape=[jax.ShapeDtypeStruct((n, D_MODEL), F32), jax.ShapeDtypeStruct((2, n), I32),
                   jax.ShapeDtypeStruct((2, n), F32)],
        compiler_params=_cparams(("parallel",)),
        name="mix_out",
    )(x2d, osb, yb, ga, gb, wa, wb, wo, n2, wr, br)


def _expert_kernel(texp_ref, nused_ref, tok_ref, tokn_ref, dst_ref,
                   x1_hbm, n2_ref, roww_ref, wg_ref, wu_ref, wd_ref, out_hbm,
                   xbuf, ybuf, wg_s, wu_s, wd_s, gsem, ssem):
    i = pl.program_id(0)
    nt = pl.num_programs(0)
    nused = nused_ref[0]
    slot = lax.rem(i, 2)
    tm = ybuf.shape[0]

    def gather_copy(idx_ref, r, sl):
        return pltpu.make_async_copy(x1_hbm.at[pl.ds(idx_ref[0, 0, r], 1)], xbuf.at[sl, pl.ds(r, 1)], gsem.at[sl])

    def scatter_copy(r):
        return pltpu.make_async_copy(ybuf.at[pl.ds(r, 1)], out_hbm.at[pl.ds(dst_ref[0, 0, r], 1)], ssem.at[0])

    def for_rows(fn):
        def body(r, _):
            fn(r)
            return 0
        lax.fori_loop(0, tm, body, 0, unroll=8)

    @pl.when(i == 0)
    def _():
        ybuf[...] = jnp.zeros_like(ybuf)
        clear = pltpu.make_async_copy(ybuf, out_hbm.at[pl.ds(out_hbm.shape[0] - tm, tm)], ssem.at[0])
        clear.start()
        clear.wait()
        for_rows(lambda r: gather_copy(tok_ref, r, 0).start())

    @pl.when(i + 1 < nused)
    def _():
        for_rows(lambda r: gather_copy(tokn_ref, r, 1 - slot).start())

    @pl.when(i < nused)
    def _():
        @pl.when(jnp.logical_or(i == 0, texp_ref[i] != texp_ref[jnp.maximum(i - 1, 0)]))
        def _():
            wg_s[...] = wg_ref[0].astype(BF16)
            wu_s[...] = wu_ref[0].astype(BF16)
            wd_s[...] = wd_ref[0].astype(BF16)

        for_rows(lambda r: gather_copy(tok_ref, r, slot).wait())
        h = _rms_rows(xbuf[slot], n2_ref[...]).astype(BF16)
        a = _dot(h, wg_s[...])
        u = _dot(h, wu_s[...])
        hb = (a * jax.nn.sigmoid(a) * u).astype(BF16)
        y = _dot(hb, wd_s[...]) * roww_ref[...]

        @pl.when(i >= 1)
        def _():
            for_rows(lambda r: scatter_copy(r).wait())

        ybuf[...] = y
        for_rows(lambda r: scatter_copy(r).start())

        @pl.when(i == nt - 1)
        def _():
            for_rows(lambda r: scatter_copy(r).wait())

    @pl.when(i == nused)
    def _():
        for_rows(lambda r: scatter_copy(r).wait())


def _experts(x1, n2, meta, wg, wu, wd, tm):
    n = x1.shape[0]
    texp, nused, row_tok, row_dst, row_w = meta
    nt = texp.shape[0]
    tok3 = row_tok.reshape(nt, 1, tm)
    dst3 = row_dst.reshape(nt, 1, tm)
    smem_blk = lambda f: pl.BlockSpec((1, 1, tm), f, memory_space=pltpu.SMEM)
    return pl.pallas_call(
        _expert_kernel,
        grid_spec=pltpu.PrefetchScalarGridSpec(
            num_scalar_prefetch=2,
            grid=(nt,),
            in_specs=[smem_blk(lambda i, te, nu: (i, 0, 0)),
                      smem_blk(lambda i, te, nu: (jnp.minimum(i + 1, nt - 1), 0, 0)),
                      smem_blk(lambda i, te, nu: (i, 0, 0)),
                      pl.BlockSpec(memory_space=pl.ANY),
                      pl.BlockSpec((1, D_MODEL), lambda i, te, nu: (0, 0)),
                      pl.BlockSpec((tm, 1), lambda i, te, nu: (i, 0)),
                      pl.BlockSpec((1, D_MODEL, D_FF), lambda i, te, nu: (te[i], 0, 0)),
                      pl.BlockSpec((1, D_MODEL, D_FF), lambda i, te, nu: (te[i], 0, 0)),
                      pl.BlockSpec((1, D_FF, D_MODEL), lambda i, te, nu: (te[i], 0, 0))],
            out_specs=pl.BlockSpec(memory_space=pl.ANY),
            scratch_shapes=[pltpu.VMEM((2, tm, D_MODEL), F32),
                            pltpu.VMEM((tm, D_MODEL), F32),
                            pltpu.VMEM((D_MODEL, D_FF), BF16),
                            pltpu.VMEM((D_MODEL, D_FF), BF16),
                            pltpu.VMEM((D_FF, D_MODEL), BF16),
                            pltpu.SemaphoreType.DMA((2,)),
                            pltpu.SemaphoreType.DMA((1,))],
        ),
        out_shape=jax.ShapeDtypeStruct((2 * n + tm, D_MODEL), F32),
        compiler_params=_cparams(("arbitrary",)),
        name="experts",
    )(texp, nused, tok3, tok3, dst3, x1, n2, row_w, wg, wu, wd)


def _moe_metadata(eid, ew, n_tok, tm):
    n_assign = 2 * n_tok
    e_flat = eid.reshape(-1)
    order = jnp.argsort(e_flat).astype(I32)
    counts = jnp.sum((e_flat[:, None] == jnp.arange(N_EXPERTS, dtype=I32)[None, :]).astype(I32), axis=0)
    starts = jnp.cumsum(counts) - counts
    padded = ((counts + tm - 1) // tm) * tm
    pad_ends = jnp.cumsum(padded)
    pad_starts = pad_ends - padded
    n_tiles = n_assign // tm + N_EXPERTS
    tile_lo = jnp.arange(n_tiles, dtype=I32) * tm
    texp = jnp.minimum(jnp.sum((pad_ends[None, :] <= tile_lo[:, None]).astype(I32), axis=1), N_EXPERTS - 1)
    done = tile_lo - pad_starts[texp]
    tval = jnp.clip(counts[texp] - done, 0, tm)
    nused = (pad_ends[-1:] // tm).astype(I32)
    within = jnp.arange(tm, dtype=I32)[None, :]
    valid = (within < tval[:, None]).reshape(-1)
    src = jnp.clip((starts[texp] + done)[:, None] + within, 0, n_assign - 1).reshape(-1)
    row_asg = jnp.where(valid, order[src], 0)
    row_tok = jnp.where(row_asg >= n_tok, row_asg - n_tok, row_asg)
    row_dst = jnp.where(valid, row_asg, n_assign + jnp.broadcast_to(within, (n_tiles, tm)).reshape(-1))
    row_w = jnp.where(valid, ew.reshape(-1)[row_asg], 0.0).reshape(-1, 1)
    return texp, nused, row_tok, row_dst, row_w


def _combine_kernel(x_ref, a_ref, b_ref, o_ref):
    o_ref[...] = x_ref[...] + (a_ref[...] + b_ref[...])


def _combine(x1, buf):
    n = x1.shape[0]
    tm = min(2 * ROW_TILE, n)
    nb = n // tm
    return pl.pallas_call(
        _combine_kernel,
        grid=(nb,),
        in_specs=[pl.BlockSpec((tm, D_MODEL), lambda i: (i, 0)),
                  pl.BlockSpec((tm, D_MODEL), lambda i: (i, 0)),
                  pl.BlockSpec((tm, D_MODEL), lambda i: (i + nb, 0))],
        out_specs=pl.BlockSpec((tm, D_MODEL), lambda i: (i, 0)),
        out_shape=jax.ShapeDtypeStruct((n, D_MODEL), F32),
        compiler_params=_cparams(("parallel",)),
        name="combine",
    )(x1, buf, buf)


def _moe(x1, n2, eid, ew, wg, wu, wd):
    n = x1.shape[0]
    tm = MOE_TILE if 2 * n >= N_EXPERTS * MOE_TILE else MOE_TILE_SMALL
    buf = _experts(x1, n2, _moe_metadata(eid, ew, n, tm), wg, wu, wd, tm)
    return _combine(x1, buf)


def _state_to_pairs(s):
    b = s.shape[0]
    st = jnp.swapaxes(s, -1, -2).reshape(b, N_PAIRS, 2, HEAD_DIM, HEAD_DIM)
    z = jnp.zeros_like(st[:, :, 0])
    top = jnp.concatenate([st[:, :, 0], z], axis=-1)
    bot = jnp.concatenate([z, st[:, :, 1]], axis=-1)
    return jnp.concatenate([top, bot], axis=-2)


def _pairs_to_state(st):
    b = st.shape[0]
    h0 = st[:, :, :HEAD_DIM, :HEAD_DIM]
    h1 = st[:, :, HEAD_DIM:, HEAD_DIM:]
    s = jnp.stack([h0, h1], axis=2).reshape(b, N_HEADS, HEAD_DIM, HEAD_DIM)
    return jnp.swapaxes(s, -1, -2)


def kernel(x_prompt, x_sample, cache_k, cache_v, page_table, state_hgrn, norm1_w, w_in, q_norm_w, k_norm_w, sb_bias,
           lb_logits, hg_norm_w, w_branch_a, w_branch_b, w_out, norm2_w, w_group_router, b_group_router,
           w_expert_router, b_expert_router, w_gate_exp, w_up_exp, w_down_exp):
    depth = w_in.shape[0]
    assert depth == 1, "single layer"
    l = 0
    b, t, _ = x_prompt.shape
    bd_, tq, _ = x_sample.shape
    n_p = b * t
    n_s = bd_ * tq
    assert t % HG_CHUNK == 0 and tq <= DEC_CHUNK and tq % HG_CHUNK != 0

    lbs = jnp.cumsum(jax.nn.softmax(lb_logits.astype(F32), axis=0), axis=0)
    lb = lbs[l].reshape(1, WIDTH)
    n1 = norm1_w[l].reshape(1, D_MODEL)
    n2 = norm2_w[l].reshape(1, D_MODEL)
    w_in_b = w_in[l].astype(BF16)
    qw = jnp.tile(q_norm_w[l], N_HEADS).reshape(1, WIDTH)
    kw = jnp.tile(k_norm_w[l], N_HEADS).reshape(1, WIDTH)
    hgw = jnp.tile(hg_norm_w[l], N_HEADS).reshape(1, WIDTH)
    hd = np.arange(WIDTH) // HEAD_DIM
    bdiag = jnp.asarray((hd[:, None] == hd[None, :]).astype(np.float32), BF16)
    bias2 = sb_bias[l].astype(F32) * LOG2E
    wa = w_branch_a[l].astype(BF16)
    wb = w_branch_b[l].astype(BF16)
    wo = w_out[l].astype(BF16)
    wr32 = jnp.zeros((ROUTER_ROWS, D_MODEL), F32)
    wr32 = wr32.at[0:N_GROUPS].set(w_group_router[l].T).at[16:16 + N_EXPERTS].set(w_expert_router[l].T)
    wr = wr32.astype(BF16)
    br = jnp.full((ROUTER_ROWS, 1), NEG, F32)
    br = br.at[0:N_GROUPS, 0].set(b_group_router[l]).at[16:16 + N_EXPERTS, 0].set(b_expert_router[l])
    wg = w_gate_exp[l]
    wu = w_up_exp[l]
    wd = w_down_exp[l]

    def tri2_ge(n):
        r = np.arange(n)
        tri = (r[:, None] >= r[None, :]).astype(np.float32)
        return jnp.asarray(np.concatenate([tri, tri], axis=0), BF16)

    xp = x_prompt.reshape(n_p, D_MODEL)
    q, k, kb, v, vb, hq, hf, hi, hg, ga, gb = _in_proj(xp, n1, w_in_b, bdiag, qw, kw, t_len=t)
    o_sb = _sb_prompt(q.reshape(b, t, WIDTH), kb.reshape(b, t, WIDTH), vb.reshape(b, t, WIDTH), bias2,
                      tri2_ge(min(SB_TILE, t)))
    r3 = lambda a: a.reshape(b, t, WIDTH)
    y_b, st_p = _hgrn_prompt(r3(hq), r3(hf), r3(hi), r3(hg), lb, hgw,
                             _chunk_masks(min(HG_ROWS, t), HG_CHUNK, HG_CHUNK), bdiag)
    x1, eid, ew = _mix_out(xp, o_sb.reshape(n_p, WIDTH), y_b.reshape(n_p, WIDTH), ga, gb, wa, wb, wo, n2, wr, br)
    y_prompt = _moe(x1, n2, eid, ew, wg, wu, wd).reshape(b, t, D_MODEL)
    to_kv = lambda a: jnp.transpose(a.reshape(1, b, N_HEADS, HEAD_DIM, t), (0, 1, 4, 2, 3))
    new_k_prompt = to_kv(k)
    new_v_prompt = to_kv(v)
    new_hgrn_prompt = _pairs_to_state(st_p)[None]

    xs = x_sample.reshape(n_s, D_MODEL)
    q, k, v, hq, hf, hi, hg, ga, gb = _in_proj_precise(xs, n1, w_in[l], bdiag, qw, kw)
    pad_keys = lambda a: jnp.pad(jnp.swapaxes(a.reshape(bd_, tq, WIDTH), 1, 2), ((0, 0), (0, 0), (0, PAGE - tq)))
    n_pool = cache_k.shape[1]
    pages_t = lambda c: jnp.transpose(c, (0, 2, 3, 1)).reshape(n_pool, WIDTH, PAGE)
    o_sb = _sb_decode(q.reshape(bd_, tq, WIDTH), pad_keys(k), pad_keys(v), pages_t(cache_k[l]), pages_t(cache_v[l]),
                      page_table, bias2, tri2_ge(PAGE))
    pad_rows = lambda a: jnp.pad(a.reshape(bd_, tq, WIDTH), ((0, 0), (0, DEC_CHUNK - tq), (0, 0))).reshape(
        bd_ * DEC_CHUNK, WIDTH)
    y_b, st_s = _hgrn_decode(pad_rows(hq), pad_rows(hf), pad_rows(hi), pad_rows(hg),
                             _state_to_pairs(state_hgrn[l].astype(F32)), lb, hgw,
                             _chunk_masks(bd_ * DEC_CHUNK, DEC_CHUNK, tq), bdiag)
    y_b = y_b.reshape(bd_, DEC_CHUNK, WIDTH)[:, :tq].reshape(n_s, WIDTH)
    x1, eid, ew = _mix_out(xs, o_sb.reshape(n_s, WIDTH), y_b, ga, gb, w_branch_a[l], w_branch_b[l], w_out[l], n2,
                           wr32, br, precise=True)
    y_sample = _moe(x1, n2, eid, ew, wg, wu, wd).reshape(bd_, tq, D_MODEL)
    new_k_sample = k.reshape(1, bd_, tq, N_HEADS, HEAD_DIM)
    new_v_sample = v.reshape(1, bd_, tq, N_HEADS, HEAD_DIM)
    new_hgrn_sample = _pairs_to_state(st_s)[None]

    return (y_prompt, y_sample, new_k_prompt, new_v_prompt, new_hgrn_prompt,
            new_k_sample, new_v_sample, new_hgrn_sample)
```

```python
import functools

import jax
import jax.numpy as jnp
import numpy as np
from jax import lax
from jax.experimental import pallas as pl
from jax.experimental.pallas import tpu as pltpu

F32 = jnp.float32
BF16 = jnp.bfloat16
I32 = jnp.int32

D_MODEL = 1024
N_HEADS = 8
HEAD_DIM = 64
WIDTH = N_HEADS * HEAD_DIM
LANES = 128
SUBLANES = 8
N_PAIRS = WIDTH // LANES
N_GROUPS = 4
EXPERTS_PER_GROUP = 8
N_EXPERTS = N_GROUPS * EXPERTS_PER_GROUP
D_FF = 512
PAGE = 128
HG_CHUNK = 32
RMS_EPS = 1e-6
NEG = -1e30
LOG2E = 1.4426950408889634

VMEM_LIMIT = 56 * 1024 * 1024
ROW_TILE = 256
SB_TILE = 256
SB_PAIRS = 4
DEC_GROUP = 8
HG_ROWS = 256
MOE_TILE = 256
MOE_TILE_SMALL = 32
PAGES_PER_STEP = 32
DEC_CHUNK = 16
ROUTER_ROWS = 48


def _cparams(sem):
    return pltpu.CompilerParams(dimension_semantics=sem, vmem_limit_bytes=VMEM_LIMIT)


def _dot(a, b):
    return jnp.dot(a, b, preferred_element_type=F32)


def _dot_nt(a, b):
    return lax.dot_general(a, b, (((1,), (1,)), ((), ())), preferred_element_type=F32)


def _split_bf16(x):
    hi = x.astype(BF16)
    lo = (x - hi.astype(F32)).astype(BF16)
    return hi, lo


def _prep(x, precise):
    return _split_bf16(x) if precise else (x.astype(BF16),)


def _mm(pa, pb, nt=False):
    d = _dot_nt if nt else _dot
    out = d(pa[0], pb[0])
    if len(pa) == 2:
        out = out + (d(pa[0], pb[1]) + d(pa[1], pb[0]))
    return out


def _dot_split(x, m):
    hi, lo = _split_bf16(x)
    return _dot(hi, m) + _dot(lo, m)


def _head_mean_sq(h, bd):
    return _dot_split(h * h, bd) * (1.0 / HEAD_DIM)


def _rms_rows(x, w):
    ms = jnp.mean(x * x, axis=-1, keepdims=True)
    return x * lax.rsqrt(ms + RMS_EPS) * w


def _softplus2(u):
    return jnp.maximum(u, 0.0) + jnp.log2(1.0 + jnp.exp2(-jnp.abs(u)))


def _cum_ge(sp, tri2):
    hi, lo = _split_bf16(sp)
    return _dot(jnp.concatenate([hi, lo], axis=1), tri2)


_SEGS = {"q": (0, 512), "k": (512, 1024), "v": (1024, 1536), "hq": (1536, 2048), "hf": (2048, 2560),
         "hi": (2560, 3072), "hg": (3072, 3584), "ga": (3584, 4608), "gb": (4608, 5632)}
D_IN = 5632


def _inproj_kernel(x_ref, n1_ref, w_ref, bd_ref, qw_ref, kw_ref,
                   q_ref, k_ref, kb_ref, v_ref, vb_ref, hq_ref, hf_ref, hi_ref, hg_ref, ga_ref, gb_ref, *, kv_t):
    xn = _rms_rows(x_ref[...], n1_ref[...]).astype(BF16)
    bd = bd_ref[...]

    def seg(name):
        a, b = _SEGS[name]
        return _dot(xn, w_ref[:, a:b])

    hq = seg("q")
    hk = seg("k")
    hv = seg("v")
    hq_ref[...] = seg("hq")
    hf_ref[...] = seg("hf")
    hi_ref[...] = seg("hi")
    hg_ref[...] = seg("hg")
    ga_ref[...] = seg("ga")
    gb_ref[...] = seg("gb")
    qn = hq * lax.rsqrt(_head_mean_sq(hq, bd) + RMS_EPS) * qw_ref[...]
    q_ref[...] = (qn * (HEAD_DIM ** -0.5 * LOG2E)).astype(BF16)
    kn = hk * lax.rsqrt(_head_mean_sq(hk, bd) + RMS_EPS) * kw_ref[...]
    kb_ref[...] = kn.astype(BF16)
    vb_ref[...] = hv.astype(BF16)
    if kv_t:
        k_ref[0] = kn.T
        v_ref[0] = hv.T
    else:
        k_ref[...] = kn
        v_ref[...] = hv


def _in_proj(x2d, n1, w_in, bd, qw, kw, t_len=None):
    n = x2d.shape[0]
    tm = min(ROW_TILE, n)
    row = lambda w: pl.BlockSpec((tm, w), lambda i: (i, 0))
    full = lambda a: pl.BlockSpec(a.shape, lambda i: (0,) * a.ndim)
    if t_len is None:
        kv_spec = row(WIDTH)
        kv_shape = jax.ShapeDtypeStruct((n, WIDTH), F32)
    else:
        per_seq = t_len // tm
        kv_spec = pl.BlockSpec((1, WIDTH, tm), lambda i: (i // per_seq, 0, i % per_seq))
        kv_shape = jax.ShapeDtypeStruct((n // t_len, WIDTH, t_len), F32)
    rows = lambda w, dt: (row(w), jax.ShapeDtypeStruct((n, w), dt))
    outs = [rows(WIDTH, BF16), (kv_spec, kv_shape), rows(WIDTH, BF16), (kv_spec, kv_shape), rows(WIDTH, BF16),
            rows(WIDTH, F32), rows(WIDTH, F32), rows(WIDTH, F32), rows(WIDTH, F32),
            rows(D_MODEL, F32), rows(D_MODEL, F32)]
    return pl.pallas_call(
        functools.partial(_inproj_kernel, kv_t=t_len is not None),
        grid=(n // tm,),
        in_specs=[row(D_MODEL), full(n1), full(w_in), full(bd), full(qw), full(kw)],
        out_specs=[s for s, _ in outs],
        out_shape=[s for _, s in outs],
        compiler_params=_cparams(("parallel",)),
        name="in_proj",
    )(x2d, n1, w_in, bd, qw, kw)


def _inproj_precise_kernel(x_ref, n1_ref, w_ref, bd_ref, qw_ref, kw_ref, h_ref):
    j = pl.program_id(0)
    xn = _rms_rows(x_ref[...], n1_ref[...])
    h = _mm(_split_bf16(xn), _split_bf16(w_ref[...]))
    bd = bd_ref[...]

    @pl.when(j == 0)
    def _():
        h_ref[...] = h * lax.rsqrt(_head_mean_sq(h, bd) + RMS_EPS) * qw_ref[...] * (HEAD_DIM ** -0.5 * LOG2E)

    @pl.when(j == 1)
    def _():
        h_ref[...] = h * lax.rsqrt(_head_mean_sq(h, bd) + RMS_EPS) * kw_ref[...]

    @pl.when(j >= 2)
    def _():
        h_ref[...] = h


def _in_proj_precise(x2d, n1, w_in, bd, qw, kw):
    n = x2d.shape[0]
    full = lambda a: pl.BlockSpec(a.shape, lambda j: (0,) * a.ndim)
    h = pl.pallas_call(
        _inproj_precise_kernel,
        grid=(D_IN // WIDTH,),
        in_specs=[full(x2d), full(n1), pl.BlockSpec((D_MODEL, WIDTH), lambda j: (0, j)), full(bd), full(qw), full(kw)],
        out_specs=pl.BlockSpec((n, WIDTH), lambda j: (0, j)),
        out_shape=jax.ShapeDtypeStruct((n, D_IN), F32),
        compiler_params=_cparams(("arbitrary",)),
        name="in_proj_precise",
    )(x2d, n1, w_in, bd, qw, kw)
    return [h[:, a:b] for a, b in _SEGS.values()]


def _sb_prompt_kernel(bias_ref, q_ref, k_ref, v_ref, tri2_ref, o_ref, *, tile):
    p = pl.program_id(1)
    i = pl.program_id(2)
    nh = 2 * SB_PAIRS
    lane = lax.broadcasted_iota(I32, (tile, LANES), 1)
    first = lane < HEAD_DIM
    qs = []
    for pp in range(SB_PAIRS):
        q = q_ref[0, :, pp * LANES:(pp + 1) * LANES]
        zero = jnp.zeros_like(q)
        qs += [jnp.where(first, q, zero), jnp.where(first, zero, q)]
    bias = [bias_ref[nh * p + h] for h in range(nh)]
    tri = tri2_ref[0:tile, :]
    row = lax.broadcasted_iota(I32, (tile, tile), 0)
    col = lax.broadcasted_iota(I32, (tile, tile), 1)
    causal = col < row

    def step(j, carry, accs, diag):
        start = pl.multiple_of(j * tile, tile)
        pair = lambda h: slice((h // 2) * LANES, (h // 2 + 1) * LANES)
        us = [_dot_nt(qs[h], k_ref[0, pl.ds(start, tile), pair(h)]) + bias[h] for h in range(nh)]
        cums = []
        for h in range(nh):
            sp = _softplus2(us[h])
            if diag:
                sp = jnp.where(causal, sp, 0.0)
            cums.append(_dot(sp.astype(BF16), tri) + carry[h])
        new_a = []
        for h in range(nh):
            a = jnp.exp2(us[h] - cums[h])
            if diag:
                a = jnp.where(causal, a, 0.0)
            new_a.append(accs[h] + _dot(a.astype(BF16), v_ref[0, pl.ds(start, tile), pair(h)]))
        return tuple(c[:, 0:1] for c in cums), tuple(new_a)

    zc = tuple(jnp.zeros((tile, 1), F32) for _ in range(nh))
    za = tuple(jnp.zeros((tile, LANES), F32) for _ in range(nh))
    carry, accs = step(i, zc, za, True)

    def body(n, state):
        return step(i - 1 - n, state[0], state[1], False)

    carry, accs = lax.fori_loop(0, i, body, (carry, accs))
    for pp in range(SB_PAIRS):
        o_ref[0, :, pp * LANES:(pp + 1) * LANES] = jnp.where(first, accs[2 * pp], accs[2 * pp + 1]).astype(BF16)


def _sb_prompt(q, k, v, bias2, tri2):
    b, t, _ = q.shape
    tile = tri2.shape[1]
    w = SB_PAIRS * LANES
    kern = functools.partial(_sb_prompt_kernel, tile=tile)
    return pl.pallas_call(
        kern,
        grid_spec=pltpu.PrefetchScalarGridSpec(
            num_scalar_prefetch=0,
            grid=(b, WIDTH // w, t // tile),
            in_specs=[pl.BlockSpec(memory_space=pltpu.SMEM),
                      pl.BlockSpec((1, tile, w), lambda bi, p, i: (bi, i, p)),
                      pl.BlockSpec((1, t, w), lambda bi, p, i: (bi, 0, p)),
                      pl.BlockSpec((1, t, w), lambda bi, p, i: (bi, 0, p)),
                      pl.BlockSpec((2 * tile, tile), lambda bi, p, i: (0, 0))],
            out_specs=pl.BlockSpec((1, tile, w), lambda bi, p, i: (bi, i, p)),
        ),
        out_shape=jax.ShapeDtypeStruct((b, t, WIDTH), BF16),
        compiler_params=_cparams(("parallel", "parallel", "arbitrary")),
        name="sb_prompt",
    )(bias2, q, k, v, tri2)


def _sb_decode_kernel(pt_ref, bias_ref, q_ref, knt_ref, vnt_ref, ck_hbm, cv_hbm, tri2_ref, o_ref,
                      kbuf, vbuf, sem, qbd_s, bias_s, acc_s, carry_s, *, n_pages, n_q):
    tri2 = tri2_ref[...]
    b = pl.program_id(0)
    s = pl.program_id(1)
    nb = pl.num_programs(0)
    ns = pl.num_programs(1)
    pp = kbuf.shape[1]
    g = b * ns + s
    slot = lax.rem(g, 2)
    rows = n_q * N_HEADS

    def page_copies(bb, ss, sl, i):
        page = pt_ref[bb, n_pages - (ss + 1) * pp + i]
        return (pltpu.make_async_copy(ck_hbm.at[page], kbuf.at[sl, i], sem.at[0, sl]),
                pltpu.make_async_copy(cv_hbm.at[page], vbuf.at[sl, i], sem.at[1, sl]))

    def issue(bb, ss, sl):
        for i in range(pp):
            ck, cv = page_copies(bb, ss, sl, i)
            ck.start()
            cv.start()

    @pl.when(g == 0)
    def _():
        issue(0, 0, 0)

    @pl.when(g + 1 < nb * ns)
    def _():
        last = s + 1 == ns
        issue(jnp.where(last, b + 1, b), jnp.where(last, 0, s + 1), 1 - slot)

    head_of_lane = lax.broadcasted_iota(I32, (N_HEADS, WIDTH), 1) >> 6
    head_of_row = lax.broadcasted_iota(I32, (N_HEADS, WIDTH), 0)
    head_mask = head_of_lane == head_of_row

    def tall(a2, pb, nt):
        d = _dot_nt if nt else _dot
        both = d(a2, pb[0])
        out = both[:rows] + both[rows:]
        if len(pb) == 2:
            out = out + d(a2[:rows], pb[1])
        return out

    @pl.when(s == 0)
    def _():
        qf = q_ref[0]
        for qi in range(n_q):
            blk = jnp.where(head_mask, jnp.broadcast_to(qf[qi:qi + 1, :], (N_HEADS, WIDTH)), 0.0)
            hi, lo = _split_bf16(blk)
            qbd_s[qi * N_HEADS:(qi + 1) * N_HEADS, :] = hi
            qbd_s[rows + qi * N_HEADS:rows + (qi + 1) * N_HEADS, :] = lo
        r = lax.broadcasted_iota(I32, (rows, LANES), 0) & (N_HEADS - 1)
        bv = jnp.zeros((rows, LANES), F32)
        for h in range(N_HEADS):
            bv = jnp.where(r == h, bias_ref[h], bv)
        bias_s[...] = bv
        u = tall(qbd_s[...], _split_bf16(knt_ref[0]), False) + bv
        key = lax.broadcasted_iota(I32, (rows, LANES), 1)
        qidx = lax.broadcasted_iota(I32, (rows, LANES), 0) >> 3
        mask = key < qidx
        cum = _cum_ge(jnp.where(mask, _softplus2(u), 0.0), tri2)
        a = jnp.where(mask, jnp.exp2(u - cum), 0.0)
        acc_s[...] = tall(jnp.concatenate(_split_bf16(a), axis=0), _split_bf16(vnt_ref[0]), True)
        carry_s[...] = jnp.broadcast_to(cum[:, 0:1], (rows, LANES))

    for i in range(pp):
        ck, cv = page_copies(b, s, slot, i)
        ck.wait()
        cv.wait()

    qbd = qbd_s[...]
    bias = bias_s[...]
    group = min(DEC_GROUP, pp)

    def body(n, _):
        js = [(pp // group - 1 - n) * group + (group - 1 - t) for t in range(group)]
        us = [tall(qbd, (kbuf[slot, j].astype(BF16),), False) + bias for j in js]
        sums = [_cum_ge(_softplus2(u), tri2) for u in us]
        carry = carry_s[...]
        acc = acc_s[...]
        for t, j in enumerate(js):
            cum = sums[t] + carry
            a = jnp.exp2(us[t] - cum)
            acc = acc + tall(jnp.concatenate(_split_bf16(a), axis=0), (vbuf[slot, j].astype(BF16),), True)
            carry = jnp.broadcast_to(cum[:, 0:1], (rows, LANES))
        acc_s[...] = acc
        carry_s[...] = carry
        return 0

    lax.fori_loop(0, pp // group, body, 0)

    @pl.when(s == ns - 1)
    def _():
        acc = acc_s[...]
        for qi in range(n_q):
            blk = jnp.where(head_mask, acc[qi * N_HEADS:(qi + 1) * N_HEADS, :], 0.0)
            o_ref[0, qi:qi + 1, :] = jnp.sum(blk, axis=0, keepdims=True)


def _sb_decode(q, knt, vnt, ckt, cvt, page_table, bias2, tri2):
    bd, n_q, _ = q.shape
    n_pages = page_table.shape[1]
    pp = min(PAGES_PER_STEP, n_pages)
    rows = n_q * N_HEADS
    kern = functools.partial(_sb_decode_kernel, n_pages=n_pages, n_q=n_q)
    return pl.pallas_call(
        kern,
        grid_spec=pltpu.PrefetchScalarGridSpec(
            num_scalar_prefetch=1,
            grid=(bd, n_pages // pp),
            in_specs=[pl.BlockSpec(memory_space=pltpu.SMEM),
                      pl.BlockSpec((1, n_q, WIDTH), lambda b, s, pt: (b, 0, 0)),
                      pl.BlockSpec((1, WIDTH, PAGE), lambda b, s, pt: (b, 0, 0)),
                      pl.BlockSpec((1, WIDTH, PAGE), lambda b, s, pt: (b, 0, 0)),
                      pl.BlockSpec(memory_space=pl.ANY),
                      pl.BlockSpec(memory_space=pl.ANY),
                      pl.BlockSpec((2 * PAGE, PAGE), lambda b, s, pt: (0, 0))],
            out_specs=pl.BlockSpec((1, n_q, WIDTH), lambda b, s, pt: (b, 0, 0)),
            scratch_shapes=[pltpu.VMEM((2, pp, WIDTH, PAGE), F32),
                            pltpu.VMEM((2, pp, WIDTH, PAGE), F32),
                            pltpu.SemaphoreType.DMA((2, 2)),
                            pltpu.VMEM((2 * rows, WIDTH), BF16),
                            pltpu.VMEM((rows, LANES), F32),
                            pltpu.VMEM((rows, WIDTH), F32),
                            pltpu.VMEM((rows, LANES), F32)],
        ),
        out_shape=jax.ShapeDtypeStruct((bd, n_q, WIDTH), F32),
        compiler_params=_cparams(("arbitrary", "arbitrary")),
        name="sb_decode",
    )(page_table, bias2, q, knt, vnt, ckt, cvt, tri2)


def _hgrn_rows(hq, hf, hi, lb, ltri, llast):
    f = lb + (1.0 - lb) * jax.nn.sigmoid(hf)
    g = jnp.log(f)
    kk = 1.0 - f
    g_hi, g_lo = _split_bf16(g)
    gcum = _dot(ltri, g_hi) + _dot(ltri, g_lo)
    glast = _dot(llast, g_hi) + _dot(llast, g_lo)
    q_dec = hq * jnp.exp(gcum)
    k_dec = kk * jnp.exp(-gcum)
    k_rem = kk * jnp.exp(glast - gcum)
    return q_dec, k_dec, k_rem, glast


def _hgrn_block(q_dec, k_dec, k_rem, glast, hi, cmask, chunk, chained, get_state, put_state, precise=False):
    n_rows = q_dec.shape[0]
    n_chunks = n_rows // chunk
    pairs = [slice(p * LANES, (p + 1) * LANES) for p in range(N_PAIRS)]
    first = lax.broadcasted_iota(I32, (n_rows, LANES), 1) < HEAD_DIM
    row_chunk = lax.broadcasted_iota(I32, (n_rows, LANES), 0) >> (chunk.bit_length() - 1)
    sr = lax.broadcasted_iota(I32, (LANES, LANES), 0) >> 6
    sc = lax.broadcasted_iota(I32, (LANES, LANES), 1) >> 6
    same_head = sr == sc
    prep = lambda x: _prep(x, precise)
    vbs = [prep(hi[:, sl]) for sl in pairs]
    scores = []
    for sl in pairs:
        q = q_dec[:, sl]
        kb = prep(k_dec[:, sl])
        s0 = _mm(prep(jnp.where(first, q, 0.0)), kb, nt=True)
        s1 = _mm(prep(jnp.where(first, 0.0, q)), kb, nt=True)
        scores.append((prep(jnp.where(cmask, s0, 0.0)), prep(jnp.where(cmask, s1, 0.0))))
    intra = [jnp.where(first, _mm(s0, vb), _mm(s1, vb)) for (s0, s1), vb in zip(scores, vbs)]
    upds = []
    for sl in pairs:
        vt = prep(hi[:, sl].T)
        kr = k_rem[:, sl]
        upds.append([_mm(vt, prep(jnp.where(row_chunk == c, kr, 0.0))) for c in range(n_chunks)])
    befores = []
    for p, sl in enumerate(pairs):
        gl = glast[:, sl]
        sts = []
        st = get_state(p, 0) if chained else None
        for c in range(n_chunks):
            if not chained:
                st = get_state(p, c)
            sts.append(prep(st))
            decay = jnp.exp(gl[c * chunk:c * chunk + 1, :])
            st = st * decay + jnp.where(same_head, upds[p][c], 0.0)
            if not chained:
                put_state(p, c, st)
        if chained:
            put_state(p, n_chunks - 1, st)
        befores.append(sts)
    outs = []
    for p, sl in enumerate(pairs):
        qb = prep(q_dec[:, sl])
        inter = [_mm(tuple(t[c * chunk:(c + 1) * chunk] for t in qb), befores[p][c], nt=True)
                 for c in range(n_chunks)]
        outs.append(intra[p] + jnp.concatenate(inter, axis=0))
    return jnp.concatenate(outs, axis=1)


def _hgrn_finish(o, gate, bd, hgw):
    y = o * lax.rsqrt(_head_mean_sq(o, bd) + RMS_EPS) * hgw
    return y * (gate * jax.nn.sigmoid(gate))


def _hgrn_prompt_kernel(hq_ref, hf_ref, hi_ref, hg_ref, lb_ref, hgw_ref, ltri_ref, llast_ref, cm_ref, bd_ref,
                        y_ref, st_ref):
    @pl.when(pl.program_id(1) == 0)
    def _():
        st_ref[...] = jnp.zeros_like(st_ref)

    hi = hi_ref[0]
    q_dec, k_dec, k_rem, glast = _hgrn_rows(hq_ref[0], hf_ref[0], hi, lb_ref[...], ltri_ref[...], llast_ref[...])
    cmask = cm_ref[...] > 0

    def get_state(p, c):
        return st_ref[0, p]

    def put_state(p, c, val):
        st_ref[0, p] = val

    o = _hgrn_block(q_dec, k_dec, k_rem, glast, hi, cmask, HG_CHUNK, True, get_state, put_state)
    y_ref[0] = _hgrn_finish(o, hg_ref[0], bd_ref[...], hgw_ref[...]).astype(BF16)


def _hgrn_prompt(hq, hf, hi, hg, lb, hgw, masks, bd):
    b, t, _ = hq.shape
    rows = min(HG_ROWS, t)
    ltri, llast, cm = masks
    blk = pl.BlockSpec((1, rows, WIDTH), lambda bi, j: (bi, j, 0))
    full = lambda a: pl.BlockSpec(a.shape, lambda bi, j: (0,) * a.ndim)
    return pl.pallas_call(
        _hgrn_prompt_kernel,
        grid=(b, t // rows),
        in_specs=[blk, blk, blk, blk, full(lb), full(hgw), full(ltri), full(llast), full(cm), full(bd)],
        out_specs=[blk, pl.BlockSpec((1, N_PAIRS, LANES, LANES), lambda bi, j: (bi, 0, 0, 0))],
        out_shape=[jax.ShapeDtypeStruct((b, t, WIDTH), BF16),
                   jax.ShapeDtypeStruct((b, N_PAIRS, LANES, LANES), F32)],
        compiler_params=_cparams(("parallel", "arbitrary")),
        name="hgrn_prompt",
    )(hq, hf, hi, hg, lb, hgw, ltri, llast, cm, bd)


def _hgrn_decode_kernel(hq_ref, hf_ref, hi_ref, hg_ref, s0_ref, lb_ref, hgw_ref, ltri_ref, llast_ref, cm_ref, bd_ref,
                        y_ref, st_ref):
    hi = hi_ref[...]
    q_dec, k_dec, k_rem, glast = _hgrn_rows(hq_ref[...], hf_ref[...], hi, lb_ref[...], ltri_ref[...], llast_ref[...])
    cmask = cm_ref[...] > 0

    def get_state(p, c):
        return s0_ref[c, p]

    def put_state(p, c, val):
        st_ref[c, p] = val

    o = _hgrn_block(q_dec, k_dec, k_rem, glast, hi, cmask, DEC_CHUNK, False, get_state, put_state, precise=True)
    y_ref[...] = _hgrn_finish(o, hg_ref[...], bd_ref[...], hgw_ref[...])


def _hgrn_decode(hq, hf, hi, hg, s0, lb, hgw, masks, bd):
    n = hq.shape[0]
    ltri, llast, cm = masks
    full = lambda a: pl.BlockSpec(a.shape, lambda i: (0,) * a.ndim)
    args = (hq, hf, hi, hg, s0, lb, hgw, ltri, llast, cm, bd)
    return pl.pallas_call(
        _hgrn_decode_kernel,
        grid=(1,),
        in_specs=[full(a) for a in args],
        out_specs=[pl.BlockSpec((n, WIDTH), lambda i: (0, 0)), full(s0)],
        out_shape=[jax.ShapeDtypeStruct((n, WIDTH), F32), jax.ShapeDtypeStruct(s0.shape, F32)],
        compiler_params=_cparams(("arbitrary",)),
        name="hgrn_decode",
    )(*args)


def _chunk_masks(rows, chunk, valid):
    r = np.arange(rows)
    same = (r[:, None] // chunk) == (r[None, :] // chunk)
    ltri = same & (r[None, :] <= r[:, None])
    llast = same & ((r[None, :] % chunk) < valid)
    as_bf16 = lambda m: jnp.asarray(m.astype(np.float32), BF16)
    return as_bf16(ltri), as_bf16(llast), as_bf16(ltri)


def _mixout_kernel(x_ref, osb_ref, yb_ref, ga_ref, gb_ref, wa_ref, wb_ref, wo_ref, n2_ref, wr_ref, br_ref,
                   x1_ref, eid_ref, ew_ref, *, precise):
    if precise:
        prep = lambda x: _split_bf16(x)
    else:
        prep = lambda x: (x.astype(BF16),)
    ya = _mm(prep(osb_ref[...]), prep(wa_ref[...]))
    yb = _mm(prep(yb_ref[...]), prep(wb_ref[...]))
    merged = jax.nn.sigmoid(ga_ref[...]) * ya + jax.nn.sigmoid(gb_ref[...]) * yb
    x1 = x_ref[...] + _mm(prep(merged), prep(wo_ref[...]))
    for j in range(SUBLANES):
        x1_ref[pl.ds(j, x1.shape[0], stride=SUBLANES), :] = x1[:, j * LANES:(j + 1) * LANES]
    h2 = _rms_rows(x1, n2_ref[...])
    lg = _mm(prep(wr_ref[...]), prep(h2), nt=True) + br_ref[...]
    tm = lg.shape[1]
    gl = lg[0:8]
    gmax = jnp.max(gl, axis=0, keepdims=True)
    p_group = 1.0 / jnp.sum(jnp.exp(gl - gmax), axis=0, keepdims=True)
    gr = lax.broadcasted_iota(I32, (8, tm), 0)
    gidx = jnp.min(jnp.where(gl == gmax, gr, 8), axis=0, keepdims=True)
    el = lg[16:16 + N_EXPERTS]
    er = lax.broadcasted_iota(I32, (N_EXPERTS, tm), 0)
    e1 = jnp.where((er >> 3) == gidx, el, NEG)
    m1 = jnp.max(e1, axis=0, keepdims=True)
    i1 = jnp.min(jnp.where(e1 == m1, er, N_EXPERTS), axis=0, keepdims=True)
    e2 = jnp.where(er == i1, NEG, e1)
    m2 = jnp.max(e2, axis=0, keepdims=True)
    i2 = jnp.min(jnp.where(e2 == m2, er, N_EXPERTS), axis=0, keepdims=True)
    t = jnp.exp(m2 - m1)
    w1 = p_group / (1.0 + t)
    eid_ref[...] = jnp.concatenate([i1, i2], axis=0)
    ew_ref[...] = jnp.concatenate([w1, w1 * t], axis=0)


def _mix_out(x2d, osb, yb, ga, gb, wa, wb, wo, n2, wr, br, precise=False):
    n = x2d.shape[0]
    tm = min(ROW_TILE, n)
    row = lambda w: pl.BlockSpec((tm, w), lambda i: (i, 0))
    full = lambda a: pl.BlockSpec(a.shape, lambda i: (0,) * a.ndim)
    return pl.pallas_call(
        functools.partial(_mixout_kernel, precise=precise),
        grid=(n // tm,),
        in_specs=[row(D_MODEL), row(WIDTH), row(WIDTH), row(D_MODEL), row(D_MODEL),
                  full(wa), full(wb), full(wo), full(n2), full(wr), full(br)],
        out_specs=[pl.BlockSpec((tm * SUBLANES, LANES), lambda i: (i, 0)),
                   pl.BlockSpec((2, tm), lambda i: (0, i)), pl.BlockSpec((2, tm), lambda i: (0, i))],
        out_shape=[jax.ShapeDtypeStruct((n * SUBLANES, LANES), F32), jax.ShapeDtypeStruct((2, n), I32),
                   jax.ShapeDtypeStruct((2, n), F32)],
        compiler_params=_cparams(("parallel",)),
        name="mix_out",
    )(x2d, osb, yb, ga, gb, wa, wb, wo, n2, wr, br)


def _expert_kernel(texp_ref, nused_ref, tok_ref, tokn_ref, dst_ref,
                   x1_hbm, n2_ref, roww_ref, wg_ref, wu_ref, wd_ref, out_hbm,
                   xbuf, ybuf, wg_s, wu_s, wd_s, gsem, ssem):
    i = pl.program_id(0)
    nt = pl.num_programs(0)
    nused = nused_ref[0]
    slot = lax.rem(i, 2)
    tm = ybuf.shape[0] // SUBLANES

    def tile(r):
        return pl.ds(pl.multiple_of(r * SUBLANES, SUBLANES), SUBLANES)

    def gather_copy(idx_ref, r, sl):
        return pltpu.make_async_copy(x1_hbm.at[idx_ref[0, 0, r]], xbuf.at[sl, tile(r)], gsem.at[sl])

    def scatter_copy(r):
        return pltpu.make_async_copy(ybuf.at[tile(r)], out_hbm.at[dst_ref[0, 0, r]], ssem.at[0])

    def clear_copy(r):
        return pltpu.make_async_copy(ybuf.at[tile(r)], out_hbm.at[out_hbm.shape[0] - tm + r], ssem.at[0])

    def for_rows(fn):
        def body(r, _):
            fn(r)
            return 0
        lax.fori_loop(0, tm, body, 0, unroll=8)

    @pl.when(i == 0)
    def _():
        ybuf[...] = jnp.zeros_like(ybuf)
        for_rows(lambda r: clear_copy(r).start())
        for_rows(lambda r: clear_copy(r).wait())
        for_rows(lambda r: gather_copy(tok_ref, r, 0).start())

    @pl.when(i + 1 < nused)
    def _():
        for_rows(lambda r: gather_copy(tokn_ref, r, 1 - slot).start())

    @pl.when(i < nused)
    def _():
        @pl.when(jnp.logical_or(i == 0, texp_ref[i] != texp_ref[jnp.maximum(i - 1, 0)]))
        def _():
            wg_s[...] = wg_ref[0].astype(BF16)
            wu_s[...] = wu_ref[0].astype(BF16)
            wd_s[...] = wd_ref[0].astype(BF16)

        for_rows(lambda r: gather_copy(tok_ref, r, slot).wait())
        x = jnp.concatenate([xbuf[slot, pl.ds(j, tm, stride=SUBLANES), :] for j in range(SUBLANES)], axis=1)
        h = _rms_rows(x, n2_ref[...]).astype(BF16)
        a = _dot(h, wg_s[...])
        u = _dot(h, wu_s[...])
        hb = (a * jax.nn.sigmoid(a) * u).astype(BF16)
        y = _dot(hb, wd_s[...]) * roww_ref[...]

        @pl.when(i >= 1)
        def _():
            for_rows(lambda r: scatter_copy(r).wait())

        for j in range(SUBLANES):
            ybuf[pl.ds(j, tm, stride=SUBLANES), :] = y[:, j * LANES:(j + 1) * LANES]
        for_rows(lambda r: scatter_copy(r).start())

        @pl.when(i == nt - 1)
        def _():
            for_rows(lambda r: scatter_copy(r).wait())

    @pl.when(i == nused)
    def _():
        for_rows(lambda r: scatter_copy(r).wait())


def _experts(x1t, n2, meta, wg, wu, wd, tm):
    n = x1t.shape[0]
    texp, nused, row_tok, row_dst, row_w = meta
    nt = texp.shape[0]
    tok3 = row_tok.reshape(nt, 1, tm)
    dst3 = row_dst.reshape(nt, 1, tm)
    smem_blk = lambda f: pl.BlockSpec((1, 1, tm), f, memory_space=pltpu.SMEM)
    return pl.pallas_call(
        _expert_kernel,
        grid_spec=pltpu.PrefetchScalarGridSpec(
            num_scalar_prefetch=2,
            grid=(nt,),
            in_specs=[smem_blk(lambda i, te, nu: (i, 0, 0)),
                      smem_blk(lambda i, te, nu: (jnp.minimum(i + 1, nt - 1), 0, 0)),
                      smem_blk(lambda i, te, nu: (i, 0, 0)),
                      pl.BlockSpec(memory_space=pl.ANY),
                      pl.BlockSpec((1, D_MODEL), lambda i, te, nu: (0, 0)),
                      pl.BlockSpec((tm, 1), lambda i, te, nu: (i, 0)),
                      pl.BlockSpec((1, D_MODEL, D_FF), lambda i, te, nu: (te[i], 0, 0)),
                      pl.BlockSpec((1, D_MODEL, D_FF), lambda i, te, nu: (te[i], 0, 0)),
                      pl.BlockSpec((1, D_FF, D_MODEL), lambda i, te, nu: (te[i], 0, 0))],
            out_specs=pl.BlockSpec(memory_space=pl.ANY),
            scratch_shapes=[pltpu.VMEM((2, tm * SUBLANES, LANES), F32),
                            pltpu.VMEM((tm * SUBLANES, LANES), F32),
                            pltpu.VMEM((D_MODEL, D_FF), BF16),
                            pltpu.VMEM((D_MODEL, D_FF), BF16),
                            pltpu.VMEM((D_FF, D_MODEL), BF16),
                            pltpu.SemaphoreType.DMA((2,)),
                            pltpu.SemaphoreType.DMA((1,))],
        ),
        out_shape=jax.ShapeDtypeStruct((2 * n + tm, SUBLANES, LANES), F32),
        compiler_params=_cparams(("arbitrary",)),
        name="experts",
    )(texp, nused, tok3, tok3, dst3, x1t, n2, row_w, wg, wu, wd)


def _moe_metadata(eid, ew, n_tok, tm):
    n_assign = 2 * n_tok
    e_flat = eid.reshape(-1)
    order = jnp.argsort(e_flat).astype(I32)
    counts = jnp.sum((e_flat[:, None] == jnp.arange(N_EXPERTS, dtype=I32)[None, :]).astype(I32), axis=0)
    starts = jnp.cumsum(counts) - counts
    padded = ((counts + tm - 1) // tm) * tm
    pad_ends = jnp.cumsum(padded)
    pad_starts = pad_ends - padded
    n_tiles = n_assign // tm + N_EXPERTS
    tile_lo = jnp.arange(n_tiles, dtype=I32) * tm
    texp = jnp.minimum(jnp.sum((pad_ends[None, :] <= tile_lo[:, None]).astype(I32), axis=1), N_EXPERTS - 1)
    done = tile_lo - pad_starts[texp]
    tval = jnp.clip(counts[texp] - done, 0, tm)
    nused = (pad_ends[-1:] // tm).astype(I32)
    within = jnp.arange(tm, dtype=I32)[None, :]
    valid = (within < tval[:, None]).reshape(-1)
    src = jnp.clip((starts[texp] + done)[:, None] + within, 0, n_assign - 1).reshape(-1)
    row_asg = jnp.where(valid, order[src], 0)
    row_tok = jnp.where(row_asg >= n_tok, row_asg - n_tok, row_asg)
    row_dst = jnp.where(valid, row_asg, n_assign + jnp.broadcast_to(within, (n_tiles, tm)).reshape(-1))
    row_w = jnp.where(valid, ew.reshape(-1)[row_asg], 0.0).reshape(-1, 1)
    return texp, nused, row_tok, row_dst, row_w


def _combine_kernel(x_ref, a_ref, b_ref, o_ref):
    tm = o_ref.shape[0]
    for j in range(SUBLANES):
        rows = pl.ds(j, tm, stride=SUBLANES)
        o_ref[:, j * LANES:(j + 1) * LANES] = x_ref[rows, :] + (a_ref[rows, :] + b_ref[rows, :])


def _combine(x1t, buf):
    n = x1t.shape[0] // SUBLANES
    tm = min(2 * ROW_TILE, n)
    nb = n // tm
    blk = lambda f: pl.BlockSpec((tm * SUBLANES, LANES), f)
    return pl.pallas_call(
        _combine_kernel,
        grid=(nb,),
        in_specs=[blk(lambda i: (i, 0)), blk(lambda i: (i, 0)), blk(lambda i: (i + nb, 0))],
        out_specs=pl.BlockSpec((tm, D_MODEL), lambda i: (i, 0)),
        out_shape=jax.ShapeDtypeStruct((n, D_MODEL), F32),
        compiler_params=_cparams(("parallel",)),
        name="combine",
    )(x1t, buf, buf)


def _moe(x1t, n2, eid, ew, wg, wu, wd):
    n = x1t.shape[0] // SUBLANES
    tm = MOE_TILE if 2 * n >= N_EXPERTS * MOE_TILE else MOE_TILE_SMALL
    buf = _experts(x1t.reshape(n, SUBLANES, LANES), n2, _moe_metadata(eid, ew, n, tm), wg, wu, wd, tm)
    return _combine(x1t, buf.reshape(-1, LANES))


def _state_to_pairs(s):
    b = s.shape[0]
    st = jnp.swapaxes(s, -1, -2).reshape(b, N_PAIRS, 2, HEAD_DIM, HEAD_DIM)
    z = jnp.zeros_like(st[:, :, 0])
    top = jnp.concatenate([st[:, :, 0], z], axis=-1)
    bot = jnp.concatenate([z, st[:, :, 1]], axis=-1)
    return jnp.concatenate([top, bot], axis=-2)


def _pairs_to_state(st):
    b = st.shape[0]
    h0 = st[:, :, :HEAD_DIM, :HEAD_DIM]
    h1 = st[:, :, HEAD_DIM:, HEAD_DIM:]
    s = jnp.stack([h0, h1], axis=2).reshape(b, N_HEADS, HEAD_DIM, HEAD_DIM)
    return jnp.swapaxes(s, -1, -2)


def kernel(x_prompt, x_sample, cache_k, cache_v, page_table, state_hgrn, norm1_w, w_in, q_norm_w, k_norm_w, sb_bias,
           lb_logits, hg_norm_w, w_branch_a, w_branch_b, w_out, norm2_w, w_group_router, b_group_router,
           w_expert_router, b_expert_router, w_gate_exp, w_up_exp, w_down_exp):
    depth = w_in.shape[0]
    assert depth == 1, "single layer"
    l = 0
    b, t, _ = x_prompt.shape
    bd_, tq, _ = x_sample.shape
    n_p = b * t
    n_s = bd_ * tq
    assert t % HG_CHUNK == 0 and tq <= DEC_CHUNK and tq % HG_CHUNK != 0

    lbs = jnp.cumsum(jax.nn.softmax(lb_logits.astype(F32), axis=0), axis=0)
    lb = lbs[l].reshape(1, WIDTH)
    n1 = norm1_w[l].reshape(1, D_MODEL)
    n2 = norm2_w[l].reshape(1, D_MODEL)
    w_in_b = w_in[l].astype(BF16)
    qw = jnp.tile(q_norm_w[l], N_HEADS).reshape(1, WIDTH)
    kw = jnp.tile(k_norm_w[l], N_HEADS).reshape(1, WIDTH)
    hgw = jnp.tile(hg_norm_w[l], N_HEADS).reshape(1, WIDTH)
    hd = np.arange(WIDTH) // HEAD_DIM
    bdiag = jnp.asarray((hd[:, None] == hd[None, :]).astype(np.float32), BF16)
    bias2 = sb_bias[l].astype(F32) * LOG2E
    wa = w_branch_a[l].astype(BF16)
    wb = w_branch_b[l].astype(BF16)
    wo = w_out[l].astype(BF16)
    wr32 = jnp.zeros((ROUTER_ROWS, D_MODEL), F32)
    wr32 = wr32.at[0:N_GROUPS].set(w_group_router[l].T).at[16:16 + N_EXPERTS].set(w_expert_router[l].T)
    wr = wr32.astype(BF16)
    br = jnp.full((ROUTER_ROWS, 1), NEG, F32)
    br = br.at[0:N_GROUPS, 0].set(b_group_router[l]).at[16:16 + N_EXPERTS, 0].set(b_expert_router[l])
    wg = w_gate_exp[l]
    wu = w_up_exp[l]
    wd = w_down_exp[l]

    def tri2_ge(n):
        r = np.arange(n)
        tri = (r[:, None] >= r[None, :]).astype(np.float32)
        return jnp.asarray(np.concatenate([tri, tri], axis=0), BF16)

    xp = x_prompt.reshape(n_p, D_MODEL)
    q, k, kb, v, vb, hq, hf, hi, hg, ga, gb = _in_proj(xp, n1, w_in_b, bdiag, qw, kw, t_len=t)
    o_sb = _sb_prompt(q.reshape(b, t, WIDTH), kb.reshape(b, t, WIDTH), vb.reshape(b, t, WIDTH), bias2,
                      tri2_ge(min(SB_TILE, t)))
    r3 = lambda a: a.reshape(b, t, WIDTH)
    y_b, st_p = _hgrn_prompt(r3(hq), r3(hf), r3(hi), r3(hg), lb, hgw,
                             _chunk_masks(min(HG_ROWS, t), HG_CHUNK, HG_CHUNK), bdiag)
    x1, eid, ew = _mix_out(xp, o_sb.reshape(n_p, WIDTH), y_b.reshape(n_p, WIDTH), ga, gb, wa, wb, wo, n2, wr, br)
    y_prompt = _moe(x1, n2, eid, ew, wg, wu, wd).reshape(b, t, D_MODEL)
    to_kv = lambda a: jnp.transpose(a.reshape(1, b, N_HEADS, HEAD_DIM, t), (0, 1, 4, 2, 3))
    new_k_prompt = to_kv(k)
    new_v_prompt = to_kv(v)
    new_hgrn_prompt = _pairs_to_state(st_p)[None]

    xs = x_sample.reshape(n_s, D_MODEL)
    q, k, v, hq, hf, hi, hg, ga, gb = _in_proj_precise(xs, n1, w_in[l], bdiag, qw, kw)
    pad_keys = lambda a: jnp.pad(jnp.swapaxes(a.reshape(bd_, tq, WIDTH), 1, 2), ((0, 0), (0, 0), (0, PAGE - tq)))
    n_pool = cache_k.shape[1]
    pages_t = lambda c: jnp.transpose(c, (0, 2, 3, 1)).reshape(n_pool, WIDTH, PAGE)
    o_sb = _sb_decode(q.reshape(bd_, tq, WIDTH), pad_keys(k), pad_keys(v), pages_t(cache_k[l]), pages_t(cache_v[l]),
                      page_table, bias2, tri2_ge(PAGE))
    pad_rows = lambda a: jnp.pad(a.reshape(bd_, tq, WIDTH), ((0, 0), (0, DEC_CHUNK - tq), (0, 0))).reshape(
        bd_ * DEC_CHUNK, WIDTH)
    y_b, st_s = _hgrn_decode(pad_rows(hq), pad_rows(hf), pad_rows(hi), pad_rows(hg),
                             _state_to_pairs(state_hgrn[l].astype(F32)), lb, hgw,
                             _chunk_masks(bd_ * DEC_CHUNK, DEC_CHUNK, tq), bdiag)
    y_b = y_b.reshape(bd_, DEC_CHUNK, WIDTH)[:, :tq].reshape(n_s, WIDTH)
    x1, eid, ew = _mix_out(xs, o_sb.reshape(n_s, WIDTH), y_b, ga, gb, w_branch_a[l], w_branch_b[l], w_out[l], n2,
                           wr32, br, precise=True)
    y_sample = _moe(x1, n2, eid, ew, wg, wu, wd).reshape(bd_, tq, D_MODEL)
    new_k_sample = k.reshape(1, bd_, tq, N_HEADS, HEAD_DIM)
    new_v_sample = v.reshape(1, bd_, tq, N_HEADS, HEAD_DIM)
    new_hgrn_sample = _pairs_to_state(st_s)[None]

    return (y_prompt, y_sample, new_k_prompt, new_v_prompt, new_hgrn_prompt,
            new_k_sample, new_v_sample, new_hgrn_sample)
```

```python
import functools

import jax
import jax.numpy as jnp
import numpy as np
from jax import lax
from jax.experimental import pallas as pl
from jax.experimental.pallas import tpu as pltpu

F32 = jnp.float32
BF16 = jnp.bfloat16
I32 = jnp.int32

D_MODEL = 1024
N_HEADS = 8
HEAD_DIM = 64
WIDTH = N_HEADS * HEAD_DIM
LANES = 128
SUBLANES = 8
N_PAIRS = WIDTH // LANES
N_GROUPS = 4
EXPERTS_PER_GROUP = 8
N_EXPERTS = N_GROUPS * EXPERTS_PER_GROUP
D_FF = 512
PAGE = 128
HG_CHUNK = 32
RMS_EPS = 1e-6
NEG = -1e30
LOG2E = 1.4426950408889634

VMEM_LIMIT = 56 * 1024 * 1024
ROW_TILE = 256
SB_TILE = 256
SB_PAIRS = 4
DEC_GROUP = 16
HG_ROWS = 256
MOE_TILE = 256
MOE_TILE_SMALL = 32
PAGES_PER_STEP = 32
DEC_CHUNK = 16
ROUTER_ROWS = 48


def _cparams(sem):
    return pltpu.CompilerParams(dimension_semantics=sem, vmem_limit_bytes=VMEM_LIMIT)


def _dot(a, b):
    return jnp.dot(a, b, preferred_element_type=F32)


def _dot_nt(a, b):
    return lax.dot_general(a, b, (((1,), (1,)), ((), ())), preferred_element_type=F32)


def _split_bf16(x):
    hi = x.astype(BF16)
    lo = (x - hi.astype(F32)).astype(BF16)
    return hi, lo


def _prep(x, precise):
    return _split_bf16(x) if precise else (x.astype(BF16),)


def _mm(pa, pb, nt=False):
    d = _dot_nt if nt else _dot
    out = d(pa[0], pb[0])
    if len(pa) == 2:
        out = out + (d(pa[0], pb[1]) + d(pa[1], pb[0]))
    return out


def _dot_split(x, m):
    hi, lo = _split_bf16(x)
    return _dot(hi, m) + _dot(lo, m)


def _head_mean_sq(h, bd):
    return _dot_split(h * h, bd) * (1.0 / HEAD_DIM)


def _rms_rows(x, w):
    ms = jnp.mean(x * x, axis=-1, keepdims=True)
    return x * lax.rsqrt(ms + RMS_EPS) * w


def _softplus2(u):
    return jnp.maximum(u, 0.0) + jnp.log2(1.0 + jnp.exp2(-jnp.abs(u)))


def _cum_ge(sp, tri2):
    hi, lo = _split_bf16(sp)
    return _dot(jnp.concatenate([hi, lo], axis=1), tri2)


_SEGS = {"q": (0, 512), "k": (512, 1024), "v": (1024, 1536), "hq": (1536, 2048), "hf": (2048, 2560),
         "hi": (2560, 3072), "hg": (3072, 3584), "ga": (3584, 4608), "gb": (4608, 5632)}
D_IN = 5632


def _inproj_kernel(x_ref, n1_ref, w_ref, bd_ref, qw_ref, kw_ref,
                   q_ref, k_ref, kb_ref, v_ref, vb_ref, hq_ref, hf_ref, hi_ref, hg_ref, ga_ref, gb_ref, *, kv_t):
    xn = _rms_rows(x_ref[...], n1_ref[...]).astype(BF16)
    bd = bd_ref[...]

    def seg(name):
        a, b = _SEGS[name]
        return _dot(xn, w_ref[:, a:b])

    hq = seg("q")
    hk = seg("k")
    hv = seg("v")
    hq_ref[...] = seg("hq")
    hf_ref[...] = seg("hf")
    hi_ref[...] = seg("hi")
    hg_ref[...] = seg("hg")
    ga_ref[...] = seg("ga")
    gb_ref[...] = seg("gb")
    qn = hq * lax.rsqrt(_head_mean_sq(hq, bd) + RMS_EPS) * qw_ref[...]
    q_ref[...] = (qn * (HEAD_DIM ** -0.5 * LOG2E)).astype(BF16)
    kn = hk * lax.rsqrt(_head_mean_sq(hk, bd) + RMS_EPS) * kw_ref[...]
    kb_ref[...] = kn.astype(BF16)
    vb_ref[...] = hv.astype(BF16)
    if kv_t:
        k_ref[0] = kn.T
        v_ref[0] = hv.T
    else:
        k_ref[...] = kn
        v_ref[...] = hv


def _in_proj(x2d, n1, w_in, bd, qw, kw, t_len=None):
    n = x2d.shape[0]
    tm = min(ROW_TILE, n)
    row = lambda w: pl.BlockSpec((tm, w), lambda i: (i, 0))
    full = lambda a: pl.BlockSpec(a.shape, lambda i: (0,) * a.ndim)
    if t_len is None:
        kv_spec = row(WIDTH)
        kv_shape = jax.ShapeDtypeStruct((n, WIDTH), F32)
    else:
        per_seq = t_len // tm
        kv_spec = pl.BlockSpec((1, WIDTH, tm), lambda i: (i // per_seq, 0, i % per_seq))
        kv_shape = jax.ShapeDtypeStruct((n // t_len, WIDTH, t_len), F32)
    rows = lambda w, dt: (row(w), jax.ShapeDtypeStruct((n, w), dt))
    outs = [rows(WIDTH, BF16), (kv_spec, kv_shape), rows(WIDTH, BF16), (kv_spec, kv_shape), rows(WIDTH, BF16),
            rows(WIDTH, F32), rows(WIDTH, F32), rows(WIDTH, F32), rows(WIDTH, F32),
            rows(D_MODEL, F32), rows(D_MODEL, F32)]
    return pl.pallas_call(
        functools.partial(_inproj_kernel, kv_t=t_len is not None),
        grid=(n // tm,),
        in_specs=[row(D_MODEL), full(n1), full(w_in), full(bd), full(qw), full(kw)],
        out_specs=[s for s, _ in outs],
        out_shape=[s for _, s in outs],
        compiler_params=_cparams(("parallel",)),
        name="in_proj",
    )(x2d, n1, w_in, bd, qw, kw)


def _inproj_precise_kernel(x_ref, n1_ref, w_ref, bd_ref, qw_ref, kw_ref, h_ref):
    j = pl.program_id(0)
    xn = _rms_rows(x_ref[...], n1_ref[...])
    h = _mm(_split_bf16(xn), _split_bf16(w_ref[...]))
    bd = bd_ref[...]

    @pl.when(j == 0)
    def _():
        h_ref[...] = h * lax.rsqrt(_head_mean_sq(h, bd) + RMS_EPS) * qw_ref[...] * (HEAD_DIM ** -0.5 * LOG2E)

    @pl.when(j == 1)
    def _():
        h_ref[...] = h * lax.rsqrt(_head_mean_sq(h, bd) + RMS_EPS) * kw_ref[...]

    @pl.when(j >= 2)
    def _():
        h_ref[...] = h


def _in_proj_precise(x2d, n1, w_in, bd, qw, kw):
    n = x2d.shape[0]
    full = lambda a: pl.BlockSpec(a.shape, lambda j: (0,) * a.ndim)
    h = pl.pallas_call(
        _inproj_precise_kernel,
        grid=(D_IN // WIDTH,),
        in_specs=[full(x2d), full(n1), pl.BlockSpec((D_MODEL, WIDTH), lambda j: (0, j)), full(bd), full(qw), full(kw)],
        out_specs=pl.BlockSpec((n, WIDTH), lambda j: (0, j)),
        out_shape=jax.ShapeDtypeStruct((n, D_IN), F32),
        compiler_params=_cparams(("arbitrary",)),
        name="in_proj_precise",
    )(x2d, n1, w_in, bd, qw, kw)
    return [h[:, a:b] for a, b in _SEGS.values()]


def _sb_prompt_kernel(bias_ref, q_ref, k_ref, v_ref, tri2_ref, o_ref, *, tile):
    p = pl.program_id(1)
    i = pl.program_id(2)
    nh = 2 * SB_PAIRS
    lane = lax.broadcasted_iota(I32, (tile, LANES), 1)
    first = lane < HEAD_DIM
    qs = []
    for pp in range(SB_PAIRS):
        q = q_ref[0, :, pp * LANES:(pp + 1) * LANES]
        zero = jnp.zeros_like(q)
        qs += [jnp.where(first, q, zero), jnp.where(first, zero, q)]
    bias = [bias_ref[nh * p + h] for h in range(nh)]
    tri = tri2_ref[0:tile, :]
    row = lax.broadcasted_iota(I32, (tile, tile), 0)
    col = lax.broadcasted_iota(I32, (tile, tile), 1)
    causal = col < row

    def step(j, carry, accs, diag):
        start = pl.multiple_of(j * tile, tile)
        pair = lambda h: slice((h // 2) * LANES, (h // 2 + 1) * LANES)
        us = [_dot_nt(qs[h], k_ref[0, pl.ds(start, tile), pair(h)]) + bias[h] for h in range(nh)]
        cums = []
        for h in range(nh):
            sp = _softplus2(us[h])
            if diag:
                sp = jnp.where(causal, sp, 0.0)
            cums.append(_dot(sp.astype(BF16), tri) + carry[h])
        new_a = []
        for h in range(nh):
            a = jnp.exp2(us[h] - cums[h])
            if diag:
                a = jnp.where(causal, a, 0.0)
            new_a.append(accs[h] + _dot(a.astype(BF16), v_ref[0, pl.ds(start, tile), pair(h)]))
        return tuple(c[:, 0:1] for c in cums), tuple(new_a)

    zc = tuple(jnp.zeros((tile, 1), F32) for _ in range(nh))
    za = tuple(jnp.zeros((tile, LANES), F32) for _ in range(nh))
    carry, accs = step(i, zc, za, True)

    def body(n, state):
        return step(i - 1 - n, state[0], state[1], False)

    carry, accs = lax.fori_loop(0, i, body, (carry, accs))
    for pp in range(SB_PAIRS):
        o_ref[0, :, pp * LANES:(pp + 1) * LANES] = jnp.where(first, accs[2 * pp], accs[2 * pp + 1]).astype(BF16)


def _sb_prompt(q, k, v, bias2, tri2):
    b, t, _ = q.shape
    tile = tri2.shape[1]
    w = SB_PAIRS * LANES
    kern = functools.partial(_sb_prompt_kernel, tile=tile)
    return pl.pallas_call(
        kern,
        grid_spec=pltpu.PrefetchScalarGridSpec(
            num_scalar_prefetch=0,
            grid=(b, WIDTH // w, t // tile),
            in_specs=[pl.BlockSpec(memory_space=pltpu.SMEM),
                      pl.BlockSpec((1, tile, w), lambda bi, p, i: (bi, i, p)),
                      pl.BlockSpec((1, t, w), lambda bi, p, i: (bi, 0, p)),
                      pl.BlockSpec((1, t, w), lambda bi, p, i: (bi, 0, p)),
                      pl.BlockSpec((2 * tile, tile), lambda bi, p, i: (0, 0))],
            out_specs=pl.BlockSpec((1, tile, w), lambda bi, p, i: (bi, i, p)),
        ),
        out_shape=jax.ShapeDtypeStruct((b, t, WIDTH), BF16),
        compiler_params=_cparams(("parallel", "parallel", "arbitrary")),
        name="sb_prompt",
    )(bias2, q, k, v, tri2)


def _sb_decode_kernel(pt_ref, bias_ref, q_ref, knt_ref, vnt_ref, ck_hbm, cv_hbm, tri2_ref, o_ref,
                      kbuf, vbuf, sem, qbd_s, bias_s, acc_s, carry_s, *, n_pages, n_q):
    tri2 = tri2_ref[...]
    b = pl.program_id(0)
    s = pl.program_id(1)
    nb = pl.num_programs(0)
    ns = pl.num_programs(1)
    pp = kbuf.shape[1]
    g = b * ns + s
    slot = lax.rem(g, 2)
    rows = n_q * N_HEADS

    def page_copies(bb, ss, sl, i):
        page = pt_ref[bb, n_pages - (ss + 1) * pp + i]
        return (pltpu.make_async_copy(ck_hbm.at[page], kbuf.at[sl, i], sem.at[0, sl]),
                pltpu.make_async_copy(cv_hbm.at[page], vbuf.at[sl, i], sem.at[1, sl]))

    def issue(bb, ss, sl):
        for i in range(pp):
            ck, cv = page_copies(bb, ss, sl, i)
            ck.start()
            cv.start()

    @pl.when(g == 0)
    def _():
        issue(0, 0, 0)

    @pl.when(g + 1 < nb * ns)
    def _():
        last = s + 1 == ns
        issue(jnp.where(last, b + 1, b), jnp.where(last, 0, s + 1), 1 - slot)

    head_of_lane = lax.broadcasted_iota(I32, (N_HEADS, WIDTH), 1) >> 6
    head_of_row = lax.broadcasted_iota(I32, (N_HEADS, WIDTH), 0)
    head_mask = head_of_lane == head_of_row

    def tall(a2, pb, nt):
        d = _dot_nt if nt else _dot
        both = d(a2, pb[0])
        out = both[:rows] + both[rows:]
        if len(pb) == 2:
            out = out + d(a2[:rows], pb[1])
        return out

    @pl.when(s == 0)
    def _():
        qf = q_ref[0]
        for qi in range(n_q):
            blk = jnp.where(head_mask, jnp.broadcast_to(qf[qi:qi + 1, :], (N_HEADS, WIDTH)), 0.0)
            hi, lo = _split_bf16(blk)
            qbd_s[qi * N_HEADS:(qi + 1) * N_HEADS, :] = hi
            qbd_s[rows + qi * N_HEADS:rows + (qi + 1) * N_HEADS, :] = lo
        r = lax.broadcasted_iota(I32, (rows, LANES), 0) & (N_HEADS - 1)
        bv = jnp.zeros((rows, LANES), F32)
        for h in range(N_HEADS):
            bv = jnp.where(r == h, bias_ref[h], bv)
        bias_s[...] = bv
        u = tall(qbd_s[...], _split_bf16(knt_ref[0]), False) + bv
        key = lax.broadcasted_iota(I32, (rows, LANES), 1)
        qidx = lax.broadcasted_iota(I32, (rows, LANES), 0) >> 3
        mask = key < qidx
        cum = _cum_ge(jnp.where(mask, _softplus2(u), 0.0), tri2)
        a = jnp.where(mask, jnp.exp2(u - cum), 0.0)
        acc_s[...] = tall(jnp.concatenate(_split_bf16(a), axis=0), _split_bf16(vnt_ref[0]), True)
        carry_s[...] = jnp.broadcast_to(cum[:, 0:1], (rows, LANES))

    for i in range(pp):
        ck, cv = page_copies(b, s, slot, i)
        ck.wait()
        cv.wait()

    qbd = qbd_s[...]
    bias = bias_s[...]
    group = min(DEC_GROUP, pp)

    def body(n, _):
        js = [(pp // group - 1 - n) * group + (group - 1 - t) for t in range(group)]
        us = [tall(qbd, (kbuf[slot, j].astype(BF16),), False) + bias for j in js]
        sums = [_cum_ge(_softplus2(u), tri2) for u in us]
        carry = carry_s[...]
        acc = acc_s[...]
        for t, j in enumerate(js):
            cum = sums[t] + carry
            a = jnp.exp2(us[t] - cum)
            acc = acc + tall(jnp.concatenate(_split_bf16(a), axis=0), (vbuf[slot, j].astype(BF16),), True)
            carry = jnp.broadcast_to(cum[:, 0:1], (rows, LANES))
        acc_s[...] = acc
        carry_s[...] = carry
        return 0

    lax.fori_loop(0, pp // group, body, 0)

    @pl.when(s == ns - 1)
    def _():
        acc = acc_s[...]
        for qi in range(n_q):
            blk = jnp.where(head_mask, acc[qi * N_HEADS:(qi + 1) * N_HEADS, :], 0.0)
            o_ref[0, qi:qi + 1, :] = jnp.sum(blk, axis=0, keepdims=True)


def _sb_decode(q, knt, vnt, ckt, cvt, page_table, bias2, tri2):
    bd, n_q, _ = q.shape
    n_pages = page_table.shape[1]
    pp = min(PAGES_PER_STEP, n_pages)
    rows = n_q * N_HEADS
    kern = functools.partial(_sb_decode_kernel, n_pages=n_pages, n_q=n_q)
    return pl.pallas_call(
        kern,
        grid_spec=pltpu.PrefetchScalarGridSpec(
            num_scalar_prefetch=1,
            grid=(bd, n_pages // pp),
            in_specs=[pl.BlockSpec(memory_space=pltpu.SMEM),
                      pl.BlockSpec((1, n_q, WIDTH), lambda b, s, pt: (b, 0, 0)),
                      pl.BlockSpec((1, WIDTH, PAGE), lambda b, s, pt: (b, 0, 0)),
                      pl.BlockSpec((1, WIDTH, PAGE), lambda b, s, pt: (b, 0, 0)),
                      pl.BlockSpec(memory_space=pl.ANY),
                      pl.BlockSpec(memory_space=pl.ANY),
                      pl.BlockSpec((2 * PAGE, PAGE), lambda b, s, pt: (0, 0))],
            out_specs=pl.BlockSpec((1, n_q, WIDTH), lambda b, s, pt: (b, 0, 0)),
            scratch_shapes=[pltpu.VMEM((2, pp, WIDTH, PAGE), F32),
                            pltpu.VMEM((2, pp, WIDTH, PAGE), F32),
                            pltpu.SemaphoreType.DMA((2, 2)),
                            pltpu.VMEM((2 * rows, WIDTH), BF16),
                            pltpu.VMEM((rows, LANES), F32),
                            pltpu.VMEM((rows, WIDTH), F32),
                            pltpu.VMEM((rows, LANES), F32)],
        ),
        out_shape=jax.ShapeDtypeStruct((bd, n_q, WIDTH), F32),
        compiler_params=_cparams(("arbitrary", "arbitrary")),
        name="sb_decode",
    )(page_table, bias2, q, knt, vnt, ckt, cvt, tri2)


def _hgrn_rows(hq, hf, hi, lb, ltri, llast):
    f = lb + (1.0 - lb) * jax.nn.sigmoid(hf)
    g = jnp.log(f)
    kk = 1.0 - f
    g_hi, g_lo = _split_bf16(g)
    gcum = _dot(ltri, g_hi) + _dot(ltri, g_lo)
    glast = _dot(llast, g_hi) + _dot(llast, g_lo)
    q_dec = hq * jnp.exp(gcum)
    k_dec = kk * jnp.exp(-gcum)
    k_rem = kk * jnp.exp(glast - gcum)
    return q_dec, k_dec, k_rem, glast


def _hgrn_block(q_dec, k_dec, k_rem, glast, hi, cmask, chunk, chained, get_state, put_state, precise=False):
    n_rows = q_dec.shape[0]
    n_chunks = n_rows // chunk
    pairs = [slice(p * LANES, (p + 1) * LANES) for p in range(N_PAIRS)]
    first = lax.broadcasted_iota(I32, (n_rows, LANES), 1) < HEAD_DIM
    row_chunk = lax.broadcasted_iota(I32, (n_rows, LANES), 0) >> (chunk.bit_length() - 1)
    sr = lax.broadcasted_iota(I32, (LANES, LANES), 0) >> 6
    sc = lax.broadcasted_iota(I32, (LANES, LANES), 1) >> 6
    same_head = sr == sc
    prep = lambda x: _prep(x, precise)
    vbs = [prep(hi[:, sl]) for sl in pairs]
    scores = []
    for sl in pairs:
        q = q_dec[:, sl]
        kb = prep(k_dec[:, sl])
        s0 = _mm(prep(jnp.where(first, q, 0.0)), kb, nt=True)
        s1 = _mm(prep(jnp.where(first, 0.0, q)), kb, nt=True)
        scores.append((prep(jnp.where(cmask, s0, 0.0)), prep(jnp.where(cmask, s1, 0.0))))
    intra = [jnp.where(first, _mm(s0, vb), _mm(s1, vb)) for (s0, s1), vb in zip(scores, vbs)]
    upds = []
    for sl in pairs:
        vt = prep(hi[:, sl].T)
        kr = k_rem[:, sl]
        masked = [jnp.where(row_chunk == c, kr, 0.0) for c in range(n_chunks)]
        if chained:
            wide = _mm(vt, prep(jnp.concatenate(masked, axis=1)))
            upds.append([wide[:, c * LANES:(c + 1) * LANES] for c in range(n_chunks)])
        else:
            upds.append([_mm(vt, prep(m)) for m in masked])
    befores = []
    for p, sl in enumerate(pairs):
        gl = glast[:, sl]
        sts = []
        st = get_state(p, 0) if chained else None
        for c in range(n_chunks):
            if not chained:
                st = get_state(p, c)
            sts.append(prep(st))
            decay = jnp.exp(gl[c * chunk:c * chunk + 1, :])
            st = st * decay + jnp.where(same_head, upds[p][c], 0.0)
            if not chained:
                put_state(p, c, st)
        if chained:
            put_state(p, n_chunks - 1, st)
        befores.append(sts)
    outs = []
    for p, sl in enumerate(pairs):
        qb = prep(q_dec[:, sl])
        inter = [_mm(tuple(t[c * chunk:(c + 1) * chunk] for t in qb), befores[p][c], nt=True)
                 for c in range(n_chunks)]
        outs.append(intra[p] + jnp.concatenate(inter, axis=0))
    return jnp.concatenate(outs, axis=1)


def _hgrn_finish(o, gate, bd, hgw):
    y = o * lax.rsqrt(_head_mean_sq(o, bd) + RMS_EPS) * hgw
    return y * (gate * jax.nn.sigmoid(gate))


def _hgrn_prompt_kernel(hq_ref, hf_ref, hi_ref, hg_ref, lb_ref, hgw_ref, ltri_ref, llast_ref, cm_ref, bd_ref,
                        y_ref, st_ref):
    @pl.when(pl.program_id(1) == 0)
    def _():
        st_ref[...] = jnp.zeros_like(st_ref)

    hi = hi_ref[0]
    q_dec, k_dec, k_rem, glast = _hgrn_rows(hq_ref[0], hf_ref[0], hi, lb_ref[...], ltri_ref[...], llast_ref[...])
    cmask = cm_ref[...] > 0

    def get_state(p, c):
        return st_ref[0, p]

    def put_state(p, c, val):
        st_ref[0, p] = val

    o = _hgrn_block(q_dec, k_dec, k_rem, glast, hi, cmask, HG_CHUNK, True, get_state, put_state)
    y_ref[0] = _hgrn_finish(o, hg_ref[0], bd_ref[...], hgw_ref[...]).astype(BF16)


def _hgrn_prompt(hq, hf, hi, hg, lb, hgw, masks, bd):
    b, t, _ = hq.shape
    rows = min(HG_ROWS, t)
    ltri, llast, cm = masks
    blk = pl.BlockSpec((1, rows, WIDTH), lambda bi, j: (bi, j, 0))
    full = lambda a: pl.BlockSpec(a.shape, lambda bi, j: (0,) * a.ndim)
    return pl.pallas_call(
        _hgrn_prompt_kernel,
        grid=(b, t // rows),
        in_specs=[blk, blk, blk, blk, full(lb), full(hgw), full(ltri), full(llast), full(cm), full(bd)],
        out_specs=[blk, pl.BlockSpec((1, N_PAIRS, LANES, LANES), lambda bi, j: (bi, 0, 0, 0))],
        out_shape=[jax.ShapeDtypeStruct((b, t, WIDTH), BF16),
                   jax.ShapeDtypeStruct((b, N_PAIRS, LANES, LANES), F32)],
        compiler_params=_cparams(("parallel", "arbitrary")),
        name="hgrn_prompt",
    )(hq, hf, hi, hg, lb, hgw, ltri, llast, cm, bd)


def _hgrn_decode_kernel(hq_ref, hf_ref, hi_ref, hg_ref, s0_ref, lb_ref, hgw_ref, ltri_ref, llast_ref, cm_ref, bd_ref,
                        y_ref, st_ref):
    hi = hi_ref[...]
    q_dec, k_dec, k_rem, glast = _hgrn_rows(hq_ref[...], hf_ref[...], hi, lb_ref[...], ltri_ref[...], llast_ref[...])
    cmask = cm_ref[...] > 0

    def get_state(p, c):
        return s0_ref[c, p]

    def put_state(p, c, val):
        st_ref[c, p] = val

    o = _hgrn_block(q_dec, k_dec, k_rem, glast, hi, cmask, DEC_CHUNK, False, get_state, put_state, precise=True)
    y_ref[...] = _hgrn_finish(o, hg_ref[...], bd_ref[...], hgw_ref[...])


def _hgrn_decode(hq, hf, hi, hg, s0, lb, hgw, masks, bd):
    n = hq.shape[0]
    ltri, llast, cm = masks
    full = lambda a: pl.BlockSpec(a.shape, lambda i: (0,) * a.ndim)
    args = (hq, hf, hi, hg, s0, lb, hgw, ltri, llast, cm, bd)
    return pl.pallas_call(
        _hgrn_decode_kernel,
        grid=(1,),
        in_specs=[full(a) for a in args],
        out_specs=[pl.BlockSpec((n, WIDTH), lambda i: (0, 0)), full(s0)],
        out_shape=[jax.ShapeDtypeStruct((n, WIDTH), F32), jax.ShapeDtypeStruct(s0.shape, F32)],
        compiler_params=_cparams(("arbitrary",)),
        name="hgrn_decode",
    )(*args)


def _chunk_masks(rows, chunk, valid):
    r = np.arange(rows)
    same = (r[:, None] // chunk) == (r[None, :] // chunk)
    ltri = same & (r[None, :] <= r[:, None])
    llast = same & ((r[None, :] % chunk) < valid)
    as_bf16 = lambda m: jnp.asarray(m.astype(np.float32), BF16)
    return as_bf16(ltri), as_bf16(llast), as_bf16(ltri)


def _mixout_kernel(x_ref, osb_ref, yb_ref, ga_ref, gb_ref, wa_ref, wb_ref, wo_ref, n2_ref, wr_ref, br_ref,
                   x1_ref, eid_ref, ew_ref, *, precise):
    if precise:
        prep = lambda x: _split_bf16(x)
    else:
        prep = lambda x: (x.astype(BF16),)
    ya = _mm(prep(osb_ref[...]), prep(wa_ref[...]))
    yb = _mm(prep(yb_ref[...]), prep(wb_ref[...]))
    merged = jax.nn.sigmoid(ga_ref[...]) * ya + jax.nn.sigmoid(gb_ref[...]) * yb
    x1 = x_ref[...] + _mm(prep(merged), prep(wo_ref[...]))
    for j in range(SUBLANES):
        x1_ref[pl.ds(j, x1.shape[0], stride=SUBLANES), :] = x1[:, j * LANES:(j + 1) * LANES]
    h2 = _rms_rows(x1, n2_ref[...])
    lg = _mm(prep(wr_ref[...]), prep(h2), nt=True) + br_ref[...]
    tm = lg.shape[1]
    gl = lg[0:8]
    gmax = jnp.max(gl, axis=0, keepdims=True)
    p_group = 1.0 / jnp.sum(jnp.exp(gl - gmax), axis=0, keepdims=True)
    gr = lax.broadcasted_iota(I32, (8, tm), 0)
    gidx = jnp.min(jnp.where(gl == gmax, gr, 8), axis=0, keepdims=True)
    el = lg[16:16 + N_EXPERTS]
    er = lax.broadcasted_iota(I32, (N_EXPERTS, tm), 0)
    e1 = jnp.where((er >> 3) == gidx, el, NEG)
    m1 = jnp.max(e1, axis=0, keepdims=True)
    i1 = jnp.min(jnp.where(e1 == m1, er, N_EXPERTS), axis=0, keepdims=True)
    e2 = jnp.where(er == i1, NEG, e1)
    m2 = jnp.max(e2, axis=0, keepdims=True)
    i2 = jnp.min(jnp.where(e2 == m2, er, N_EXPERTS), axis=0, keepdims=True)
    t = jnp.exp(m2 - m1)
    w1 = p_group / (1.0 + t)
    eid_ref[...] = jnp.concatenate([i1, i2], axis=0)
    ew_ref[...] = jnp.concatenate([w1, w1 * t], axis=0)


def _mix_out(x2d, osb, yb, ga, gb, wa, wb, wo, n2, wr, br, precise=False):
    n = x2d.shape[0]
    tm = min(ROW_TILE, n)
    row = lambda w: pl.BlockSpec((tm, w), lambda i: (i, 0))
    full = lambda a: pl.BlockSpec(a.shape, lambda i: (0,) * a.ndim)
    return pl.pallas_call(
        functools.partial(_mixout_kernel, precise=precise),
        grid=(n // tm,),
        in_specs=[row(D_MODEL), row(WIDTH), row(WIDTH), row(D_MODEL), row(D_MODEL),
                  full(wa), full(wb), full(wo), full(n2), full(wr), full(br)],
        out_specs=[pl.BlockSpec((tm * SUBLANES, LANES), lambda i: (i, 0)),
                   pl.BlockSpec((2, tm), lambda i: (0, i)), pl.BlockSpec((2, tm), lambda i: (0, i))],
        out_shape=[jax.ShapeDtypeStruct((n * SUBLANES, LANES), F32), jax.ShapeDtypeStruct((2, n), I32),
                   jax.ShapeDtypeStruct((2, n), F32)],
        compiler_params=_cparams(("parallel",)),
        name="mix_out",
    )(x2d, osb, yb, ga, gb, wa, wb, wo, n2, wr, br)


def _expert_kernel(texp_ref, nused_ref, tok_ref, tokn_ref, dst_ref,
                   x1_hbm, n2_ref, roww_ref, wg_ref, wu_ref, wd_ref, out_hbm,
                   xbuf, ybuf, wg_s, wu_s, wd_s, gsem, ssem):
    i = pl.program_id(0)
    nt = pl.num_programs(0)
    nused = nused_ref[0]
    slot = lax.rem(i, 2)
    tm = ybuf.shape[0] // SUBLANES

    def tile(r):
        return pl.ds(pl.multiple_of(r * SUBLANES, SUBLANES), SUBLANES)

    def gather_copy(idx_ref, r, sl):
        return pltpu.make_async_copy(x1_hbm.at[idx_ref[0, 0, r]], xbuf.at[sl, tile(r)], gsem.at[sl])

    def scatter_copy(r):
        return pltpu.make_async_copy(ybuf.at[tile(r)], out_hbm.at[dst_ref[0, 0, r]], ssem.at[0])

    def clear_copy(r):
        return pltpu.make_async_copy(ybuf.at[tile(r)], out_hbm.at[out_hbm.shape[0] - tm + r], ssem.at[0])

    def for_rows(fn):
        def body(r, _):
            fn(r)
            return 0
        lax.fori_loop(0, tm, body, 0, unroll=8)

    @pl.when(i == 0)
    def _():
        ybuf[...] = jnp.zeros_like(ybuf)
        for_rows(lambda r: clear_copy(r).start())
        for_rows(lambda r: clear_copy(r).wait())
        for_rows(lambda r: gather_copy(tok_ref, r, 0).start())

    @pl.when(i + 1 < nused)
    def _():
        for_rows(lambda r: gather_copy(tokn_ref, r, 1 - slot).start())

    @pl.when(i < nused)
    def _():
        @pl.when(jnp.logical_or(i == 0, texp_ref[i] != texp_ref[jnp.maximum(i - 1, 0)]))
        def _():
            wg_s[...] = wg_ref[0].astype(BF16)
            wu_s[...] = wu_ref[0].astype(BF16)
            wd_s[...] = wd_ref[0].astype(BF16)

        for_rows(lambda r: gather_copy(tok_ref, r, slot).wait())
        x = jnp.concatenate([xbuf[slot, pl.ds(j, tm, stride=SUBLANES), :] for j in range(SUBLANES)], axis=1)
        h = _rms_rows(x, n2_ref[...]).astype(BF16)
        a = _dot(h, wg_s[...])
        u = _dot(h, wu_s[...])
        hb = (a * jax.nn.sigmoid(a) * u).astype(BF16)
        y = _dot(hb, wd_s[...]) * roww_ref[...]

        @pl.when(i >= 1)
        def _():
            for_rows(lambda r: scatter_copy(r).wait())

        for j in range(SUBLANES):
            ybuf[pl.ds(j, tm, stride=SUBLANES), :] = y[:, j * LANES:(j + 1) * LANES]
        for_rows(lambda r: scatter_copy(r).start())

        @pl.when(i == nt - 1)
        def _():
            for_rows(lambda r: scatter_copy(r).wait())

    @pl.when(i == nused)
    def _():
        for_rows(lambda r: scatter_copy(r).wait())


def _experts(x1t, n2, meta, wg, wu, wd, tm):
    n = x1t.shape[0]
    texp, nused, row_tok, row_dst, row_w = meta
    nt = texp.shape[0]
    tok3 = row_tok.reshape(nt, 1, tm)
    dst3 = row_dst.reshape(nt, 1, tm)
    smem_blk = lambda f: pl.BlockSpec((1, 1, tm), f, memory_space=pltpu.SMEM)
    return pl.pallas_call(
        _expert_kernel,
        grid_spec=pltpu.PrefetchScalarGridSpec(
            num_scalar_prefetch=2,
            grid=(nt,),
            in_specs=[smem_blk(lambda i, te, nu: (i, 0, 0)),
                      smem_blk(lambda i, te, nu: (jnp.minimum(i + 1, nt - 1), 0, 0)),
                      smem_blk(lambda i, te, nu: (i, 0, 0)),
                      pl.BlockSpec(memory_space=pl.ANY),
                      pl.BlockSpec((1, D_MODEL), lambda i, te, nu: (0, 0)),
                      pl.BlockSpec((tm, 1), lambda i, te, nu: (i, 0)),
                      pl.BlockSpec((1, D_MODEL, D_FF), lambda i, te, nu: (te[i], 0, 0)),
                      pl.BlockSpec((1, D_MODEL, D_FF), lambda i, te, nu: (te[i], 0, 0)),
                      pl.BlockSpec((1, D_FF, D_MODEL), lambda i, te, nu: (te[i], 0, 0))],
            out_specs=pl.BlockSpec(memory_space=pl.ANY),
            scratch_shapes=[pltpu.VMEM((2, tm * SUBLANES, LANES), F32),
                            pltpu.VMEM((tm * SUBLANES, LANES), F32),
                            pltpu.VMEM((D_MODEL, D_FF), BF16),
                            pltpu.VMEM((D_MODEL, D_FF), BF16),
                            pltpu.VMEM((D_FF, D_MODEL), BF16),
                            pltpu.SemaphoreType.DMA((2,)),
                            pltpu.SemaphoreType.DMA((1,))],
        ),
        out_shape=jax.ShapeDtypeStruct((2 * n + tm, SUBLANES, LANES), F32),
        compiler_params=_cparams(("arbitrary",)),
        name="experts",
    )(texp, nused, tok3, tok3, dst3, x1t, n2, row_w, wg, wu, wd)


def _moe_metadata(eid, ew, n_tok, tm):
    n_assign = 2 * n_tok
    e_flat = eid.reshape(-1)
    order = jnp.argsort(e_flat).astype(I32)
    counts = jnp.sum((e_flat[:, None] == jnp.arange(N_EXPERTS, dtype=I32)[None, :]).astype(I32), axis=0)
    starts = jnp.cumsum(counts) - counts
    padded = ((counts + tm - 1) // tm) * tm
    pad_ends = jnp.cumsum(padded)
    pad_starts = pad_ends - padded
    n_tiles = n_assign // tm + N_EXPERTS
    tile_lo = jnp.arange(n_tiles, dtype=I32) * tm
    texp = jnp.minimum(jnp.sum((pad_ends[None, :] <= tile_lo[:, None]).astype(I32), axis=1), N_EXPERTS - 1)
    done = tile_lo - pad_starts[texp]
    tval = jnp.clip(counts[texp] - done, 0, tm)
    nused = (pad_ends[-1:] // tm).astype(I32)
    within = jnp.arange(tm, dtype=I32)[None, :]
    valid = (within < tval[:, None]).reshape(-1)
    src = jnp.clip((starts[texp] + done)[:, None] + within, 0, n_assign - 1).reshape(-1)
    row_asg = jnp.where(valid, order[src], 0)
    row_tok = jnp.where(row_asg >= n_tok, row_asg - n_tok, row_asg)
    row_dst = jnp.where(valid, row_asg, n_assign + jnp.broadcast_to(within, (n_tiles, tm)).reshape(-1))
    row_w = jnp.where(valid, ew.reshape(-1)[row_asg], 0.0).reshape(-1, 1)
    return texp, nused, row_tok, row_dst, row_w


def _combine_kernel(x_ref, a_ref, b_ref, o_ref):
    tm = o_ref.shape[0]
    for j in range(SUBLANES):
        rows = pl.ds(j, tm, stride=SUBLANES)
        o_ref[:, j * LANES:(j + 1) * LANES] = x_ref[rows, :] + (a_ref[rows, :] + b_ref[rows, :])


def _combine(x1t, buf):
    n = x1t.shape[0] // SUBLANES
    tm = min(2 * ROW_TILE, n)
    nb = n // tm
    blk = lambda f: pl.BlockSpec((tm * SUBLANES, LANES), f)
    return pl.pallas_call(
        _combine_kernel,
        grid=(nb,),
        in_specs=[blk(lambda i: (i, 0)), blk(lambda i: (i, 0)), blk(lambda i: (i + nb, 0))],
        out_specs=pl.BlockSpec((tm, D_MODEL), lambda i: (i, 0)),
        out_shape=jax.ShapeDtypeStruct((n, D_MODEL), F32),
        compiler_params=_cparams(("parallel",)),
        name="combine",
    )(x1t, buf, buf)


def _moe(x1t, n2, eid, ew, wg, wu, wd):
    n = x1t.shape[0] // SUBLANES
    tm = MOE_TILE if 2 * n >= N_EXPERTS * MOE_TILE else MOE_TILE_SMALL
    buf = _experts(x1t.reshape(n, SUBLANES, LANES), n2, _moe_metadata(eid, ew, n, tm), wg, wu, wd, tm)
    return _combine(x1t, buf.reshape(-1, LANES))


def _state_to_pairs(s):
    b = s.shape[0]
    st = jnp.swapaxes(s, -1, -2).reshape(b, N_PAIRS, 2, HEAD_DIM, HEAD_DIM)
    z = jnp.zeros_like(st[:, :, 0])
    top = jnp.concatenate([st[:, :, 0], z], axis=-1)
    bot = jnp.concatenate([z, st[:, :, 1]], axis=-1)
    return jnp.concatenate([top, bot], axis=-2)


def _pairs_to_state(st):
    b = st.shape[0]
    h0 = st[:, :, :HEAD_DIM, :HEAD_DIM]
    h1 = st[:, :, HEAD_DIM:, HEAD_DIM:]
    s = jnp.stack([h0, h1], axis=2).reshape(b, N_HEADS, HEAD_DIM, HEAD_DIM)
    return jnp.swapaxes(s, -1, -2)


def kernel(x_prompt, x_sample, cache_k, cache_v, page_table, state_hgrn, norm1_w, w_in, q_norm_w, k_norm_w, sb_bias,
           lb_logits, hg_norm_w, w_branch_a, w_branch_b, w_out, norm2_w, w_group_router, b_group_router,
           w_expert_router, b_expert_router, w_gate_exp, w_up_exp, w_down_exp):
    depth = w_in.shape[0]
    assert depth == 1, "single layer"
    l = 0
    b, t, _ = x_prompt.shape
    bd_, tq, _ = x_sample.shape
    n_p = b * t
    n_s = bd_ * tq
    assert t % HG_CHUNK == 0 and tq <= DEC_CHUNK and tq % HG_CHUNK != 0

    lbs = jnp.cumsum(jax.nn.softmax(lb_logits.astype(F32), axis=0), axis=0)
    lb = lbs[l].reshape(1, WIDTH)
    n1 = norm1_w[l].reshape(1, D_MODEL)
    n2 = norm2_w[l].reshape(1, D_MODEL)
    w_in_b = w_in[l].astype(BF16)
    qw = jnp.tile(q_norm_w[l], N_HEADS).reshape(1, WIDTH)
    kw = jnp.tile(k_norm_w[l], N_HEADS).reshape(1, WIDTH)
    hgw = jnp.tile(hg_norm_w[l], N_HEADS).reshape(1, WIDTH)
    hd = np.arange(WIDTH) // HEAD_DIM
    bdiag = jnp.asarray((hd[:, None] == hd[None, :]).astype(np.float32), BF16)
    bias2 = sb_bias[l].astype(F32) * LOG2E
    wa = w_branch_a[l].astype(BF16)
    wb = w_branch_b[l].astype(BF16)
    wo = w_out[l].astype(BF16)
    wr32 = jnp.zeros((ROUTER_ROWS, D_MODEL), F32)
    wr32 = wr32.at[0:N_GROUPS].set(w_group_router[l].T).at[16:16 + N_EXPERTS].set(w_expert_router[l].T)
    wr = wr32.astype(BF16)
    br = jnp.full((ROUTER_ROWS, 1), NEG, F32)
    br = br.at[0:N_GROUPS, 0].set(b_group_router[l]).at[16:16 + N_EXPERTS, 0].set(b_expert_router[l])
    wg = w_gate_exp[l]
    wu = w_up_exp[l]
    wd = w_down_exp[l]

    def tri2_ge(n):
        r = np.arange(n)
        tri = (r[:, None] >= r[None, :]).astype(np.float32)
        return jnp.asarray(np.concatenate([tri, tri], axis=0), BF16)

    xp = x_prompt.reshape(n_p, D_MODEL)
    q, k, kb, v, vb, hq, hf, hi, hg, ga, gb = _in_proj(xp, n1, w_in_b, bdiag, qw, kw, t_len=t)
    o_sb = _sb_prompt(q.reshape(b, t, WIDTH), kb.reshape(b, t, WIDTH), vb.reshape(b, t, WIDTH), bias2,
                      tri2_ge(min(SB_TILE, t)))
    r3 = lambda a: a.reshape(b, t, WIDTH)
    y_b, st_p = _hgrn_prompt(r3(hq), r3(hf), r3(hi), r3(hg), lb, hgw,
                             _chunk_masks(min(HG_ROWS, t), HG_CHUNK, HG_CHUNK), bdiag)
    x1, eid, ew = _mix_out(xp, o_sb.reshape(n_p, WIDTH), y_b.reshape(n_p, WIDTH), ga, gb, wa, wb, wo, n2, wr, br)
    y_prompt = _moe(x1, n2, eid, ew, wg, wu, wd).reshape(b, t, D_MODEL)
    to_kv = lambda a: jnp.transpose(a.reshape(1, b, N_HEADS, HEAD_DIM, t), (0, 1, 4, 2, 3))
    new_k_prompt = to_kv(k)
    new_v_prompt = to_kv(v)
    new_hgrn_prompt = _pairs_to_state(st_p)[None]

    xs = x_sample.reshape(n_s, D_MODEL)
    q, k, v, hq, hf, hi, hg, ga, gb = _in_proj_precise(xs, n1, w_in[l], bdiag, qw, kw)
    pad_keys = lambda a: jnp.pad(jnp.swapaxes(a.reshape(bd_, tq, WIDTH), 1, 2), ((0, 0), (0, 0), (0, PAGE - tq)))
    n_pool = cache_k.shape[1]
    pages_t = lambda c: jnp.transpose(c, (0, 2, 3, 1)).reshape(n_pool, WIDTH, PAGE)
    o_sb = _sb_decode(q.reshape(bd_, tq, WIDTH), pad_keys(k), pad_keys(v), pages_t(cache_k[l]), pages_t(cache_v[l]),
                      page_table, bias2, tri2_ge(PAGE))
    pad_rows = lambda a: jnp.pad(a.reshape(bd_, tq, WIDTH), ((0, 0), (0, DEC_CHUNK - tq), (0, 0))).reshape(
        bd_ * DEC_CHUNK, WIDTH)
    y_b, st_s = _hgrn_decode(pad_rows(hq), pad_rows(hf), pad_rows(hi), pad_rows(hg),
                             _state_to_pairs(state_hgrn[l].astype(F32)), lb, hgw,
                             _chunk_masks(bd_ * DEC_CHUNK, DEC_CHUNK, tq), bdiag)
    y_b = y_b.reshape(bd_, DEC_CHUNK, WIDTH)[:, :tq].reshape(n_s, WIDTH)
    x1, eid, ew = _mix_out(xs, o_sb.reshape(n_s, WIDTH), y_b, ga, gb, w_branch_a[l], w_branch_b[l], w_out[l], n2,
                           wr32, br, precise=True)
    y_sample = _moe(x1, n2, eid, ew, wg, wu, wd).reshape(bd_, tq, D_MODEL)
    new_k_sample = k.reshape(1, bd_, tq, N_HEADS, HEAD_DIM)
    new_v_sample = v.reshape(1, bd_, tq, N_HEADS, HEAD_DIM)
    new_hgrn_sample = _pairs_to_state(st_s)[None]

    return (y_prompt, y_sample, new_k_prompt, new_v_prompt, new_hgrn_prompt,
            new_k_sample, new_v_sample, new_hgrn_sample)
```

```python
import functools

import jax
import jax.numpy as jnp
import numpy as np
from jax import lax
from jax.experimental import pallas as pl
from jax.experimental.pallas import tpu as pltpu

F32 = jnp.float32
BF16 = jnp.bfloat16
I32 = jnp.int32

D_MODEL = 1024
N_HEADS = 8
HEAD_DIM = 64
WIDTH = N_HEADS * HEAD_DIM
LANES = 128
SUBLANES = 8
N_PAIRS = WIDTH // LANES
N_GROUPS = 4
EXPERTS_PER_GROUP = 8
N_EXPERTS = N_GROUPS * EXPERTS_PER_GROUP
D_FF = 512
PAGE = 128
HG_CHUNK = 32
RMS_EPS = 1e-6
NEG = -1e30
LOG2E = 1.4426950408889634

VMEM_LIMIT = 56 * 1024 * 1024
ROW_TILE = 256
SB_TILE = 256
SB_PAIRS = 4
DEC_GROUP = 16
HG_ROWS = 256
MOE_TILE = 256
MOE_TILE_SMALL = 32
PAGES_PER_STEP = 32
DEC_CHUNK = 16
ROUTER_ROWS = 48


def _cparams(sem):
    return pltpu.CompilerParams(dimension_semantics=sem, vmem_limit_bytes=VMEM_LIMIT)


def _dot(a, b):
    return jnp.dot(a, b, preferred_element_type=F32)


def _dot_nt(a, b):
    return lax.dot_general(a, b, (((1,), (1,)), ((), ())), preferred_element_type=F32)


def _split_bf16(x):
    hi = x.astype(BF16)
    lo = (x - hi.astype(F32)).astype(BF16)
    return hi, lo


def _prep(x, precise):
    return _split_bf16(x) if precise else (x.astype(BF16),)


def _mm(pa, pb, nt=False):
    d = _dot_nt if nt else _dot
    out = d(pa[0], pb[0])
    if len(pa) == 2:
        out = out + (d(pa[0], pb[1]) + d(pa[1], pb[0]))
    return out


def _dot_split(x, m):
    hi, lo = _split_bf16(x)
    return _dot(hi, m) + _dot(lo, m)


def _head_mean_sq(h, bd):
    return _dot_split(h * h, bd) * (1.0 / HEAD_DIM)


def _rms_rows(x, w):
    ms = jnp.mean(x * x, axis=-1, keepdims=True)
    return x * lax.rsqrt(ms + RMS_EPS) * w


def _softplus2(u):
    return jnp.maximum(u, 0.0) + jnp.log2(1.0 + jnp.exp2(-jnp.abs(u)))


def _cum_ge(sp, tri2):
    hi, lo = _split_bf16(sp)
    return _dot(jnp.concatenate([hi, lo], axis=1), tri2)


_SEGS = {"q": (0, 512), "k": (512, 1024), "v": (1024, 1536), "hq": (1536, 2048), "hf": (2048, 2560),
         "hi": (2560, 3072), "hg": (3072, 3584), "ga": (3584, 4608), "gb": (4608, 5632)}
D_IN = 5632


def _inproj_kernel(x_ref, n1_ref, w_ref, bd_ref, qw_ref, kw_ref,
                   q_ref, k_ref, kb_ref, v_ref, vb_ref, hq_ref, hf_ref, hi_ref, hg_ref, ga_ref, gb_ref, *, kv_t):
    xn = _rms_rows(x_ref[...], n1_ref[...]).astype(BF16)
    bd = bd_ref[...]

    def seg(name):
        a, b = _SEGS[name]
        return _dot(xn, w_ref[:, a:b])

    hq = seg("q")
    hk = seg("k")
    hv = seg("v")
    hq_ref[...] = seg("hq")
    hf_ref[...] = seg("hf")
    hi_ref[...] = seg("hi")
    hg_ref[...] = seg("hg")
    ga_ref[...] = seg("ga")
    gb_ref[...] = seg("gb")
    qn = hq * lax.rsqrt(_head_mean_sq(hq, bd) + RMS_EPS) * qw_ref[...]
    q_ref[...] = (qn * (HEAD_DIM ** -0.5 * LOG2E)).astype(BF16)
    kn = hk * lax.rsqrt(_head_mean_sq(hk, bd) + RMS_EPS) * kw_ref[...]
    kb_ref[...] = kn.astype(BF16)
    vb_ref[...] = hv.astype(BF16)
    if kv_t:
        k_ref[0] = kn.T
        v_ref[0] = hv.T
    else:
        k_ref[...] = kn
        v_ref[...] = hv


def _in_proj(x2d, n1, w_in, bd, qw, kw, t_len=None):
    n = x2d.shape[0]
    tm = min(ROW_TILE, n)
    row = lambda w: pl.BlockSpec((tm, w), lambda i: (i, 0))
    full = lambda a: pl.BlockSpec(a.shape, lambda i: (0,) * a.ndim)
    if t_len is None:
        kv_spec = row(WIDTH)
        kv_shape = jax.ShapeDtypeStruct((n, WIDTH), F32)
    else:
        per_seq = t_len // tm
        kv_spec = pl.BlockSpec((1, WIDTH, tm), lambda i: (i // per_seq, 0, i % per_seq))
        kv_shape = jax.ShapeDtypeStruct((n // t_len, WIDTH, t_len), F32)
    rows = lambda w, dt: (row(w), jax.ShapeDtypeStruct((n, w), dt))
    outs = [rows(WIDTH, BF16), (kv_spec, kv_shape), rows(WIDTH, BF16), (kv_spec, kv_shape), rows(WIDTH, BF16),
            rows(WIDTH, F32), rows(WIDTH, F32), rows(WIDTH, F32), rows(WIDTH, F32),
            rows(D_MODEL, F32), rows(D_MODEL, F32)]
    return pl.pallas_call(
        functools.partial(_inproj_kernel, kv_t=t_len is not None),
        grid=(n // tm,),
        in_specs=[row(D_MODEL), full(n1), full(w_in), full(bd), full(qw), full(kw)],
        out_specs=[s for s, _ in outs],
        out_shape=[s for _, s in outs],
        compiler_params=_cparams(("parallel",)),
        name="in_proj",
    )(x2d, n1, w_in, bd, qw, kw)


def _inproj_precise_kernel(x_ref, n1_ref, w_ref, bd_ref, qw_ref, kw_ref, h_ref):
    j = pl.program_id(0)
    xn = _rms_rows(x_ref[...], n1_ref[...])
    h = _mm(_split_bf16(xn), _split_bf16(w_ref[...]))
    bd = bd_ref[...]

    @pl.when(j == 0)
    def _():
        h_ref[...] = h * lax.rsqrt(_head_mean_sq(h, bd) + RMS_EPS) * qw_ref[...] * (HEAD_DIM ** -0.5 * LOG2E)

    @pl.when(j == 1)
    def _():
        h_ref[...] = h * lax.rsqrt(_head_mean_sq(h, bd) + RMS_EPS) * kw_ref[...]

    @pl.when(j >= 2)
    def _():
        h_ref[...] = h


def _in_proj_precise(x2d, n1, w_in, bd, qw, kw):
    n = x2d.shape[0]
    full = lambda a: pl.BlockSpec(a.shape, lambda j: (0,) * a.ndim)
    h = pl.pallas_call(
        _inproj_precise_kernel,
        grid=(D_IN // WIDTH,),
        in_specs=[full(x2d), full(n1), pl.BlockSpec((D_MODEL, WIDTH), lambda j: (0, j)), full(bd), full(qw), full(kw)],
        out_specs=pl.BlockSpec((n, WIDTH), lambda j: (0, j)),
        out_shape=jax.ShapeDtypeStruct((n, D_IN), F32),
        compiler_params=_cparams(("arbitrary",)),
        name="in_proj_precise",
    )(x2d, n1, w_in, bd, qw, kw)
    return [h[:, a:b] for a, b in _SEGS.values()]


def _sb_prompt_kernel(bias_ref, q_ref, k_ref, v_ref, tri2_ref, o_ref, *, tile):
    p = pl.program_id(1)
    i = pl.program_id(2)
    nh = 2 * SB_PAIRS
    lane = lax.broadcasted_iota(I32, (tile, LANES), 1)
    first = lane < HEAD_DIM
    qs = []
    for pp in range(SB_PAIRS):
        q = q_ref[0, :, pp * LANES:(pp + 1) * LANES]
        zero = jnp.zeros_like(q)
        qs += [jnp.where(first, q, zero), jnp.where(first, zero, q)]
    bias = [bias_ref[nh * p + h] for h in range(nh)]
    tri = tri2_ref[0:tile, :]
    row = lax.broadcasted_iota(I32, (tile, tile), 0)
    col = lax.broadcasted_iota(I32, (tile, tile), 1)
    causal = col < row

    def step(j, carry, accs, diag):
        start = pl.multiple_of(j * tile, tile)
        pair = lambda h: slice((h // 2) * LANES, (h // 2 + 1) * LANES)
        us = [_dot_nt(qs[h], k_ref[0, pl.ds(start, tile), pair(h)]) + bias[h] for h in range(nh)]
        cums = []
        for h in range(nh):
            sp = _softplus2(us[h])
            if diag:
                sp = jnp.where(causal, sp, 0.0)
            cums.append(_dot(sp.astype(BF16), tri) + carry[h])
        new_a = []
        for h in range(nh):
            a = jnp.exp2(us[h] - cums[h])
            if diag:
                a = jnp.where(causal, a, 0.0)
            new_a.append(accs[h] + _dot(a.astype(BF16), v_ref[0, pl.ds(start, tile), pair(h)]))
        return tuple(c[:, 0:1] for c in cums), tuple(new_a)

    zc = tuple(jnp.zeros((tile, 1), F32) for _ in range(nh))
    za = tuple(jnp.zeros((tile, LANES), F32) for _ in range(nh))
    carry, accs = step(i, zc, za, True)

    def body(n, state):
        return step(i - 1 - n, state[0], state[1], False)

    carry, accs = lax.fori_loop(0, i, body, (carry, accs))
    for pp in range(SB_PAIRS):
        o_ref[0, :, pp * LANES:(pp + 1) * LANES] = jnp.where(first, accs[2 * pp], accs[2 * pp + 1]).astype(BF16)


def _sb_prompt(q, k, v, bias2, tri2):
    b, t, _ = q.shape
    tile = tri2.shape[1]
    w = SB_PAIRS * LANES
    kern = functools.partial(_sb_prompt_kernel, tile=tile)
    return pl.pallas_call(
        kern,
        grid_spec=pltpu.PrefetchScalarGridSpec(
            num_scalar_prefetch=0,
            grid=(b, WIDTH // w, t // tile),
            in_specs=[pl.BlockSpec(memory_space=pltpu.SMEM),
                      pl.BlockSpec((1, tile, w), lambda bi, p, i: (bi, i, p)),
                      pl.BlockSpec((1, t, w), lambda bi, p, i: (bi, 0, p)),
                      pl.BlockSpec((1, t, w), lambda bi, p, i: (bi, 0, p)),
                      pl.BlockSpec((2 * tile, tile), lambda bi, p, i: (0, 0))],
            out_specs=pl.BlockSpec((1, tile, w), lambda bi, p, i: (bi, i, p)),
        ),
        out_shape=jax.ShapeDtypeStruct((b, t, WIDTH), BF16),
        compiler_params=_cparams(("parallel", "parallel", "arbitrary")),
        name="sb_prompt",
    )(bias2, q, k, v, tri2)


def _sb_decode_kernel(pt_ref, bias_ref, q_ref, knt_ref, vnt_ref, ck_hbm, cv_hbm, tri2_ref, o_ref,
                      kbuf, vbuf, sem, qbd_s, bias_s, acc_s, carry_s, *, n_pages, n_q):
    tri2 = tri2_ref[...]
    b = pl.program_id(0)
    s = pl.program_id(1)
    nb = pl.num_programs(0)
    ns = pl.num_programs(1)
    pp = kbuf.shape[1]
    g = b * ns + s
    slot = lax.rem(g, 2)
    rows = n_q * N_HEADS

    def page_copies(bb, ss, sl, i):
        page = pt_ref[bb, n_pages - (ss + 1) * pp + i]
        return (pltpu.make_async_copy(ck_hbm.at[page], kbuf.at[sl, i], sem.at[0, sl]),
                pltpu.make_async_copy(cv_hbm.at[page], vbuf.at[sl, i], sem.at[1, sl]))

    def issue(bb, ss, sl):
        for i in range(pp):
            ck, cv = page_copies(bb, ss, sl, i)
            ck.start()
            cv.start()

    @pl.when(g == 0)
    def _():
        issue(0, 0, 0)

    @pl.when(g + 1 < nb * ns)
    def _():
        last = s + 1 == ns
        issue(jnp.where(last, b + 1, b), jnp.where(last, 0, s + 1), 1 - slot)

    head_of_lane = lax.broadcasted_iota(I32, (N_HEADS, WIDTH), 1) >> 6
    head_of_row = lax.broadcasted_iota(I32, (N_HEADS, WIDTH), 0)
    head_mask = head_of_lane == head_of_row

    def tall(a2, pb, nt):
        d = _dot_nt if nt else _dot
        both = d(a2, pb[0])
        out = both[:rows] + both[rows:]
        if len(pb) == 2:
            out = out + d(a2[:rows], pb[1])
        return out

    @pl.when(s == 0)
    def _():
        qf = q_ref[0]
        for qi in range(n_q):
            blk = jnp.where(head_mask, jnp.broadcast_to(qf[qi:qi + 1, :], (N_HEADS, WIDTH)), 0.0)
            hi, lo = _split_bf16(blk)
            qbd_s[qi * N_HEADS:(qi + 1) * N_HEADS, :] = hi
            qbd_s[rows + qi * N_HEADS:rows + (qi + 1) * N_HEADS, :] = lo
        r = lax.broadcasted_iota(I32, (rows, LANES), 0) & (N_HEADS - 1)
        bv = jnp.zeros((rows, LANES), F32)
        for h in range(N_HEADS):
            bv = jnp.where(r == h, bias_ref[h], bv)
        bias_s[...] = bv
        u = tall(qbd_s[...], _split_bf16(knt_ref[0]), False) + bv
        key = lax.broadcasted_iota(I32, (rows, LANES), 1)
        qidx = lax.broadcasted_iota(I32, (rows, LANES), 0) >> 3
        mask = key < qidx
        cum = _cum_ge(jnp.where(mask, _softplus2(u), 0.0), tri2)
        a = jnp.where(mask, jnp.exp2(u - cum), 0.0)
        acc_s[...] = tall(jnp.concatenate(_split_bf16(a), axis=0), _split_bf16(vnt_ref[0]), True)
        carry_s[...] = jnp.broadcast_to(cum[:, 0:1], (rows, LANES))

    for i in range(pp):
        ck, cv = page_copies(b, s, slot, i)
        ck.wait()
        cv.wait()

    qbd = qbd_s[...]
    bias = bias_s[...]
    group = min(DEC_GROUP, pp)

    def body(n, _):
        js = [(pp // group - 1 - n) * group + (group - 1 - t) for t in range(group)]
        us = [tall(qbd, (kbuf[slot, j].astype(BF16),), False) + bias for j in js]
        sums = [_cum_ge(_softplus2(u), tri2) for u in us]
        carry = carry_s[...]
        acc = acc_s[...]
        for t, j in enumerate(js):
            cum = sums[t] + carry
            a = jnp.exp2(us[t] - cum)
            acc = acc + tall(jnp.concatenate(_split_bf16(a), axis=0), (vbuf[slot, j].astype(BF16),), True)
            carry = jnp.broadcast_to(cum[:, 0:1], (rows, LANES))
        acc_s[...] = acc
        carry_s[...] = carry
        return 0

    lax.fori_loop(0, pp // group, body, 0)

    @pl.when(s == ns - 1)
    def _():
        acc = acc_s[...]
        for qi in range(n_q):
            blk = jnp.where(head_mask, acc[qi * N_HEADS:(qi + 1) * N_HEADS, :], 0.0)
            o_ref[0, qi:qi + 1, :] = jnp.sum(blk, axis=0, keepdims=True)


def _sb_decode(q, knt, vnt, ckt, cvt, page_table, bias2, tri2):
    bd, n_q, _ = q.shape
    n_pages = page_table.shape[1]
    pp = min(PAGES_PER_STEP, n_pages)
    rows = n_q * N_HEADS
    kern = functools.partial(_sb_decode_kernel, n_pages=n_pages, n_q=n_q)
    return pl.pallas_call(
        kern,
        grid_spec=pltpu.PrefetchScalarGridSpec(
            num_scalar_prefetch=1,
            grid=(bd, n_pages // pp),
            in_specs=[pl.BlockSpec(memory_space=pltpu.SMEM),
                      pl.BlockSpec((1, n_q, WIDTH), lambda b, s, pt: (b, 0, 0)),
                      pl.BlockSpec((1, WIDTH, PAGE), lambda b, s, pt: (b, 0, 0)),
                      pl.BlockSpec((1, WIDTH, PAGE), lambda b, s, pt: (b, 0, 0)),
                      pl.BlockSpec(memory_space=pl.ANY),
                      pl.BlockSpec(memory_space=pl.ANY),
                      pl.BlockSpec((2 * PAGE, PAGE), lambda b, s, pt: (0, 0))],
            out_specs=pl.BlockSpec((1, n_q, WIDTH), lambda b, s, pt: (b, 0, 0)),
            scratch_shapes=[pltpu.VMEM((2, pp, WIDTH, PAGE), F32),
                            pltpu.VMEM((2, pp, WIDTH, PAGE), F32),
                            pltpu.SemaphoreType.DMA((2, 2)),
                            pltpu.VMEM((2 * rows, WIDTH), BF16),
                            pltpu.VMEM((rows, LANES), F32),
                            pltpu.VMEM((rows, WIDTH), F32),
                            pltpu.VMEM((rows, LANES), F32)],
        ),
        out_shape=jax.ShapeDtypeStruct((bd, n_q, WIDTH), F32),
        compiler_params=_cparams(("arbitrary", "arbitrary")),
        name="sb_decode",
    )(page_table, bias2, q, knt, vnt, ckt, cvt, tri2)


def _hgrn_rows(hq, hf, hi, lb, ltri, llast):
    f = lb + (1.0 - lb) * jax.nn.sigmoid(hf)
    g = jnp.log(f)
    kk = 1.0 - f
    g_hi, g_lo = _split_bf16(g)
    gcum = _dot(ltri, g_hi) + _dot(ltri, g_lo)
    glast = _dot(llast, g_hi) + _dot(llast, g_lo)
    q_dec = hq * jnp.exp(gcum)
    k_dec = kk * jnp.exp(-gcum)
    k_rem = kk * jnp.exp(glast - gcum)
    return q_dec, k_dec, k_rem, glast


def _hgrn_block(q_dec, k_dec, k_rem, glast, hi, cmask, chunk, chained, get_state, put_state, precise=False):
    n_rows = q_dec.shape[0]
    n_chunks = n_rows // chunk
    pairs = [slice(p * LANES, (p + 1) * LANES) for p in range(N_PAIRS)]
    first = lax.broadcasted_iota(I32, (n_rows, LANES), 1) < HEAD_DIM
    row_chunk = lax.broadcasted_iota(I32, (n_rows, LANES), 0) >> (chunk.bit_length() - 1)
    sr = lax.broadcasted_iota(I32, (LANES, LANES), 0) >> 6
    sc = lax.broadcasted_iota(I32, (LANES, LANES), 1) >> 6
    same_head = sr == sc
    prep = lambda x: _prep(x, precise)
    vbs = [prep(hi[:, sl]) for sl in pairs]
    scores = []
    for sl in pairs:
        q = q_dec[:, sl]
        kb = prep(k_dec[:, sl])
        s0 = _mm(prep(jnp.where(first, q, 0.0)), kb, nt=True)
        s1 = _mm(prep(jnp.where(first, 0.0, q)), kb, nt=True)
        scores.append((prep(jnp.where(cmask, s0, 0.0)), prep(jnp.where(cmask, s1, 0.0))))
    intra = [jnp.where(first, _mm(s0, vb), _mm(s1, vb)) for (s0, s1), vb in zip(scores, vbs)]
    upds = []
    for sl in pairs:
        vt = prep(hi[:, sl].T)
        kr = k_rem[:, sl]
        masked = [jnp.where(row_chunk == c, kr, 0.0) for c in range(n_chunks)]
        if chained:
            wide = _mm(vt, prep(jnp.concatenate(masked, axis=1)))
            upds.append([wide[:, c * LANES:(c + 1) * LANES] for c in range(n_chunks)])
        else:
            upds.append([_mm(vt, prep(m)) for m in masked])
    befores = []
    for p, sl in enumerate(pairs):
        gl = glast[:, sl]
        sts = []
        st = get_state(p, 0) if chained else None
        for c in range(n_chunks):
            if not chained:
                st = get_state(p, c)
            sts.append(prep(st))
            decay = jnp.exp(gl[c * chunk:c * chunk + 1, :])
            st = st * decay + jnp.where(same_head, upds[p][c], 0.0)
            if not chained:
                put_state(p, c, st)
        if chained:
            put_state(p, n_chunks - 1, st)
        befores.append(sts)
    outs = []
    for p, sl in enumerate(pairs):
        qb = prep(q_dec[:, sl])
        inter = [_mm(tuple(t[c * chunk:(c + 1) * chunk] for t in qb), befores[p][c], nt=True)
                 for c in range(n_chunks)]
        outs.append(intra[p] + jnp.concatenate(inter, axis=0))
    return jnp.concatenate(outs, axis=1)


def _hgrn_finish(o, gate, bd, hgw):
    y = o * lax.rsqrt(_head_mean_sq(o, bd) + RMS_EPS) * hgw
    return y * (gate * jax.nn.sigmoid(gate))


def _hgrn_prompt_kernel(hq_ref, hf_ref, hi_ref, hg_ref, lb_ref, hgw_ref, ltri_ref, llast_ref, cm_ref, bd_ref,
                        y_ref, st_ref):
    @pl.when(pl.program_id(1) == 0)
    def _():
        st_ref[...] = jnp.zeros_like(st_ref)

    hi = hi_ref[0]
    q_dec, k_dec, k_rem, glast = _hgrn_rows(hq_ref[0], hf_ref[0], hi, lb_ref[...], ltri_ref[...], llast_ref[...])
    cmask = cm_ref[...] > 0

    def get_state(p, c):
        return st_ref[0, p]

    def put_state(p, c, val):
        st_ref[0, p] = val

    o = _hgrn_block(q_dec, k_dec, k_rem, glast, hi, cmask, HG_CHUNK, True, get_state, put_state)
    y_ref[0] = _hgrn_finish(o, hg_ref[0], bd_ref[...], hgw_ref[...]).astype(BF16)


def _hgrn_prompt(hq, hf, hi, hg, lb, hgw, masks, bd):
    b, t, _ = hq.shape
    rows = min(HG_ROWS, t)
    ltri, llast, cm = masks
    blk = pl.BlockSpec((1, rows, WIDTH), lambda bi, j: (bi, j, 0))
    full = lambda a: pl.BlockSpec(a.shape, lambda bi, j: (0,) * a.ndim)
    return pl.pallas_call(
        _hgrn_prompt_kernel,
        grid=(b, t // rows),
        in_specs=[blk, blk, blk, blk, full(lb), full(hgw), full(ltri), full(llast), full(cm), full(bd)],
        out_specs=[blk, pl.BlockSpec((1, N_PAIRS, LANES, LANES), lambda bi, j: (bi, 0, 0, 0))],
        out_shape=[jax.ShapeDtypeStruct((b, t, WIDTH), BF16),
                   jax.ShapeDtypeStruct((b, N_PAIRS, LANES, LANES), F32)],
        compiler_params=_cparams(("parallel", "arbitrary")),
        name="hgrn_prompt",
    )(hq, hf, hi, hg, lb, hgw, ltri, llast, cm, bd)


def _hgrn_decode_kernel(hq_ref, hf_ref, hi_ref, hg_ref, s0_ref, lb_ref, hgw_ref, ltri_ref, llast_ref, cm_ref, bd_ref,
                        y_ref, st_ref):
    hi = hi_ref[...]
    q_dec, k_dec, k_rem, glast = _hgrn_rows(hq_ref[...], hf_ref[...], hi, lb_ref[...], ltri_ref[...], llast_ref[...])
    cmask = cm_ref[...] > 0

    def get_state(p, c):
        return s0_ref[c, p]

    def put_state(p, c, val):
        st_ref[c, p] = val

    o = _hgrn_block(q_dec, k_dec, k_rem, glast, hi, cmask, DEC_CHUNK, False, get_state, put_state, precise=True)
    y_ref[...] = _hgrn_finish(o, hg_ref[...], bd_ref[...], hgw_ref[...])


def _hgrn_decode(hq, hf, hi, hg, s0, lb, hgw, masks, bd):
    n = hq.shape[0]
    ltri, llast, cm = masks
    full = lambda a: pl.BlockSpec(a.shape, lambda i: (0,) * a.ndim)
    args = (hq, hf, hi, hg, s0, lb, hgw, ltri, llast, cm, bd)
    return pl.pallas_call(
        _hgrn_decode_kernel,
        grid=(1,),
        in_specs=[full(a) for a in args],
        out_specs=[pl.BlockSpec((n, WIDTH), lambda i: (0, 0)), full(s0)],
        out_shape=[jax.ShapeDtypeStruct((n, WIDTH), F32), jax.ShapeDtypeStruct(s0.shape, F32)],
        compiler_params=_cparams(("arbitrary",)),
        name="hgrn_decode",
    )(*args)


def _chunk_masks(rows, chunk, valid):
    r = np.arange(rows)
    same = (r[:, None] // chunk) == (r[None, :] // chunk)
    ltri = same & (r[None, :] <= r[:, None])
    llast = same & ((r[None, :] % chunk) < valid)
    as_bf16 = lambda m: jnp.asarray(m.astype(np.float32), BF16)
    return as_bf16(ltri), as_bf16(llast), as_bf16(ltri)


def _mixout_kernel(x_ref, osb_ref, yb_ref, ga_ref, gb_ref, wa_ref, wb_ref, wo_ref, n2_ref, wr_ref, br_ref,
                   x1_ref, eid_ref, ew_ref, *, precise):
    if precise:
        prep = lambda x: _split_bf16(x)
    else:
        prep = lambda x: (x.astype(BF16),)
    ya = _mm(prep(osb_ref[...]), prep(wa_ref[...]))
    yb = _mm(prep(yb_ref[...]), prep(wb_ref[...]))
    merged = jax.nn.sigmoid(ga_ref[...]) * ya + jax.nn.sigmoid(gb_ref[...]) * yb
    x1 = x_ref[...] + _mm(prep(merged), prep(wo_ref[...]))
    for j in range(SUBLANES):
        x1_ref[pl.ds(j, x1.shape[0], stride=SUBLANES), :] = x1[:, j * LANES:(j + 1) * LANES]
    h2 = _rms_rows(x1, n2_ref[...])
    lg = _mm(prep(wr_ref[...]), prep(h2), nt=True) + br_ref[...]
    tm = lg.shape[1]
    gl = lg[0:8]
    gmax = jnp.max(gl, axis=0, keepdims=True)
    p_group = 1.0 / jnp.sum(jnp.exp(gl - gmax), axis=0, keepdims=True)
    gr = lax.broadcasted_iota(I32, (8, tm), 0)
    gidx = jnp.min(jnp.where(gl == gmax, gr, 8), axis=0, keepdims=True)
    el = lg[16:16 + N_EXPERTS]
    er = lax.broadcasted_iota(I32, (N_EXPERTS, tm), 0)
    e1 = jnp.where((er >> 3) == gidx, el, NEG)
    m1 = jnp.max(e1, axis=0, keepdims=True)
    i1 = jnp.min(jnp.where(e1 == m1, er, N_EXPERTS), axis=0, keepdims=True)
    e2 = jnp.where(er == i1, NEG, e1)
    m2 = jnp.max(e2, axis=0, keepdims=True)
    i2 = jnp.min(jnp.where(e2 == m2, er, N_EXPERTS), axis=0, keepdims=True)
    t = jnp.exp(m2 - m1)
    w1 = p_group / (1.0 + t)
    eid_ref[...] = jnp.concatenate([i1, i2], axis=0)
    ew_ref[...] = jnp.concatenate([w1, w1 * t], axis=0)


def _mix_out(x2d, osb, yb, ga, gb, wa, wb, wo, n2, wr, br, precise=False):
    n = x2d.shape[0]
    tm = min(ROW_TILE, n)
    row = lambda w: pl.BlockSpec((tm, w), lambda i: (i, 0))
    full = lambda a: pl.BlockSpec(a.shape, lambda i: (0,) * a.ndim)
    return pl.pallas_call(
        functools.partial(_mixout_kernel, precise=precise),
        grid=(n // tm,),
        in_specs=[row(D_MODEL), row(WIDTH), row(WIDTH), row(D_MODEL), row(D_MODEL),
                  full(wa), full(wb), full(wo), full(n2), full(wr), full(br)],
        out_specs=[pl.BlockSpec((tm * SUBLANES, LANES), lambda i: (i, 0)),
                   pl.BlockSpec((2, tm), lambda i: (0, i)), pl.BlockSpec((2, tm), lambda i: (0, i))],
        out_shape=[jax.ShapeDtypeStruct((n * SUBLANES, LANES), F32), jax.ShapeDtypeStruct((2, n), I32),
                   jax.ShapeDtypeStruct((2, n), F32)],
        compiler_params=_cparams(("parallel",)),
        name="mix_out",
    )(x2d, osb, yb, ga, gb, wa, wb, wo, n2, wr, br)


def _expert_kernel(texp_ref, nused_ref, tok_ref, tokn_ref, dst_ref,
                   x1_hbm, n2_ref, roww_ref, wg_ref, wu_ref, wd_ref, out_hbm,
                   xbuf, ybuf, wg_s, wu_s, wd_s, gsem, ssem):
    i = pl.program_id(0)
    nt = pl.num_programs(0)
    nused = nused_ref[0]
    slot = lax.rem(i, 2)
    tm = ybuf.shape[0] // SUBLANES

    def tile(r):
        return pl.ds(pl.multiple_of(r * SUBLANES, SUBLANES), SUBLANES)

    def gather_copy(idx_ref, r, sl):
        return pltpu.make_async_copy(x1_hbm.at[idx_ref[0, 0, r]], xbuf.at[sl, tile(r)], gsem.at[sl])

    def scatter_copy(r):
        return pltpu.make_async_copy(ybuf.at[tile(r)], out_hbm.at[dst_ref[0, 0, r]], ssem.at[0])

    def clear_copy(r):
        return pltpu.make_async_copy(ybuf.at[tile(r)], out_hbm.at[out_hbm.shape[0] - tm + r], ssem.at[0])

    def for_rows(fn):
        def body(r, _):
            fn(r)
            return 0
        lax.fori_loop(0, tm, body, 0, unroll=8)

    def start_rows(make_copy):
        def body(g, _):
            for k in range(SUBLANES):
                make_copy(g * SUBLANES + k).start(priority=k % 2)
            return 0
        lax.fori_loop(0, tm // SUBLANES, body, 0)

    @pl.when(i == 0)
    def _():
        ybuf[...] = jnp.zeros_like(ybuf)
        for_rows(lambda r: clear_copy(r).start())
        for_rows(lambda r: clear_copy(r).wait())
        start_rows(lambda r: gather_copy(tok_ref, r, 0))

    @pl.when(i + 1 < nused)
    def _():
        start_rows(lambda r: gather_copy(tokn_ref, r, 1 - slot))

    @pl.when(i < nused)
    def _():
        @pl.when(jnp.logical_or(i == 0, texp_ref[i] != texp_ref[jnp.maximum(i - 1, 0)]))
        def _():
            wg_s[...] = wg_ref[0].astype(BF16)
            wu_s[...] = wu_ref[0].astype(BF16)
            wd_s[...] = wd_ref[0].astype(BF16)

        for_rows(lambda r: gather_copy(tok_ref, r, slot).wait())
        x = jnp.concatenate([xbuf[slot, pl.ds(j, tm, stride=SUBLANES), :] for j in range(SUBLANES)], axis=1)
        h = _rms_rows(x, n2_ref[...]).astype(BF16)
        a = _dot(h, wg_s[...])
        u = _dot(h, wu_s[...])
        hb = (a * jax.nn.sigmoid(a) * u).astype(BF16)
        y = _dot(hb, wd_s[...]) * roww_ref[...]

        @pl.when(i >= 1)
        def _():
            for_rows(lambda r: scatter_copy(r).wait())

        for j in range(SUBLANES):
            ybuf[pl.ds(j, tm, stride=SUBLANES), :] = y[:, j * LANES:(j + 1) * LANES]
        start_rows(scatter_copy)

        @pl.when(i == nt - 1)
        def _():
            for_rows(lambda r: scatter_copy(r).wait())

    @pl.when(i == nused)
    def _():
        for_rows(lambda r: scatter_copy(r).wait())


def _experts(x1t, n2, meta, wg, wu, wd, tm):
    n = x1t.shape[0]
    texp, nused, row_tok, row_dst, row_w = meta
    nt = texp.shape[0]
    tok3 = row_tok.reshape(nt, 1, tm)
    dst3 = row_dst.reshape(nt, 1, tm)
    smem_blk = lambda f: pl.BlockSpec((1, 1, tm), f, memory_space=pltpu.SMEM)
    return pl.pallas_call(
        _expert_kernel,
        grid_spec=pltpu.PrefetchScalarGridSpec(
            num_scalar_prefetch=2,
            grid=(nt,),
            in_specs=[smem_blk(lambda i, te, nu: (i, 0, 0)),
                      smem_blk(lambda i, te, nu: (jnp.minimum(i + 1, nt - 1), 0, 0)),
                      smem_blk(lambda i, te, nu: (i, 0, 0)),
                      pl.BlockSpec(memory_space=pl.ANY),
                      pl.BlockSpec((1, D_MODEL), lambda i, te, nu: (0, 0)),
                      pl.BlockSpec((tm, 1), lambda i, te, nu: (i, 0)),
                      pl.BlockSpec((1, D_MODEL, D_FF), lambda i, te, nu: (te[i], 0, 0)),
                      pl.BlockSpec((1, D_MODEL, D_FF), lambda i, te, nu: (te[i], 0, 0)),
                      pl.BlockSpec((1, D_FF, D_MODEL), lambda i, te, nu: (te[i], 0, 0))],
            out_specs=pl.BlockSpec(memory_space=pl.ANY),
            scratch_shapes=[pltpu.VMEM((2, tm * SUBLANES, LANES), F32),
                            pltpu.VMEM((tm * SUBLANES, LANES), F32),
                            pltpu.VMEM((D_MODEL, D_FF), BF16),
                            pltpu.VMEM((D_MODEL, D_FF), BF16),
                            pltpu.VMEM((D_FF, D_MODEL), BF16),
                            pltpu.SemaphoreType.DMA((2,)),
                            pltpu.SemaphoreType.DMA((1,))],
        ),
        out_shape=jax.ShapeDtypeStruct((2 * n + tm, SUBLANES, LANES), F32),
        compiler_params=_cparams(("arbitrary",)),
        name="experts",
    )(texp, nused, tok3, tok3, dst3, x1t, n2, row_w, wg, wu, wd)


def _moe_metadata(eid, ew, n_tok, tm):
    n_assign = 2 * n_tok
    e_flat = eid.reshape(-1)
    order = jnp.argsort(e_flat).astype(I32)
    counts = jnp.sum((e_flat[:, None] == jnp.arange(N_EXPERTS, dtype=I32)[None, :]).astype(I32), axis=0)
    starts = jnp.cumsum(counts) - counts
    padded = ((counts + tm - 1) // tm) * tm
    pad_ends = jnp.cumsum(padded)
    pad_starts = pad_ends - padded
    n_tiles = n_assign // tm + N_EXPERTS
    tile_lo = jnp.arange(n_tiles, dtype=I32) * tm
    texp = jnp.minimum(jnp.sum((pad_ends[None, :] <= tile_lo[:, None]).astype(I32), axis=1), N_EXPERTS - 1)
    done = tile_lo - pad_starts[texp]
    tval = jnp.clip(counts[texp] - done, 0, tm)
    nused = (pad_ends[-1:] // tm).astype(I32)
    within = jnp.arange(tm, dtype=I32)[None, :]
    valid = (within < tval[:, None]).reshape(-1)
    src = jnp.clip((starts[texp] + done)[:, None] + within, 0, n_assign - 1).reshape(-1)
    row_asg = jnp.where(valid, order[src], 0)
    row_tok = jnp.where(row_asg >= n_tok, row_asg - n_tok, row_asg)
    row_dst = jnp.where(valid, row_asg, n_assign + jnp.broadcast_to(within, (n_tiles, tm)).reshape(-1))
    row_w = jnp.where(valid, ew.reshape(-1)[row_asg], 0.0).reshape(-1, 1)
    return texp, nused, row_tok, row_dst, row_w


def _combine_kernel(x_ref, a_ref, b_ref, o_ref):
    tm = o_ref.shape[0]
    for j in range(SUBLANES):
        rows = pl.ds(j, tm, stride=SUBLANES)
        o_ref[:, j * LANES:(j + 1) * LANES] = x_ref[rows, :] + (a_ref[rows, :] + b_ref[rows, :])


def _combine(x1t, buf):
    n = x1t.shape[0] // SUBLANES
    tm = min(2 * ROW_TILE, n)
    nb = n // tm
    blk = lambda f: pl.BlockSpec((tm * SUBLANES, LANES), f)
    return pl.pallas_call(
        _combine_kernel,
        grid=(nb,),
        in_specs=[blk(lambda i: (i, 0)), blk(lambda i: (i, 0)), blk(lambda i: (i + nb, 0))],
        out_specs=pl.BlockSpec((tm, D_MODEL), lambda i: (i, 0)),
        out_shape=jax.ShapeDtypeStruct((n, D_MODEL), F32),
        compiler_params=_cparams(("parallel",)),
        name="combine",
    )(x1t, buf, buf)


def _moe(x1t, n2, eid, ew, wg, wu, wd):
    n = x1t.shape[0] // SUBLANES
    tm = MOE_TILE if 2 * n >= N_EXPERTS * MOE_TILE else MOE_TILE_SMALL
    buf = _experts(x1t.reshape(n, SUBLANES, LANES), n2, _moe_metadata(eid, ew, n, tm), wg, wu, wd, tm)
    return _combine(x1t, buf.reshape(-1, LANES))


def _state_to_pairs(s):
    b = s.shape[0]
    st = jnp.swapaxes(s, -1, -2).reshape(b, N_PAIRS, 2, HEAD_DIM, HEAD_DIM)
    z = jnp.zeros_like(st[:, :, 0])
    top = jnp.concatenate([st[:, :, 0], z], axis=-1)
    bot = jnp.concatenate([z, st[:, :, 1]], axis=-1)
    return jnp.concatenate([top, bot], axis=-2)


def _pairs_to_state(st):
    b = st.shape[0]
    h0 = st[:, :, :HEAD_DIM, :HEAD_DIM]
    h1 = st[:, :, HEAD_DIM:, HEAD_DIM:]
    s = jnp.stack([h0, h1], axis=2).reshape(b, N_HEADS, HEAD_DIM, HEAD_DIM)
    return jnp.swapaxes(s, -1, -2)


def kernel(x_prompt, x_sample, cache_k, cache_v, page_table, state_hgrn, norm1_w, w_in, q_norm_w, k_norm_w, sb_bias,
           lb_logits, hg_norm_w, w_branch_a, w_branch_b, w_out, norm2_w, w_group_router, b_group_router,
           w_expert_router, b_expert_router, w_gate_exp, w_up_exp, w_down_exp):
    depth = w_in.shape[0]
    assert depth == 1, "single layer"
    l = 0
    b, t, _ = x_prompt.shape
    bd_, tq, _ = x_sample.shape
    n_p = b * t
    n_s = bd_ * tq
    assert t % HG_CHUNK == 0 and tq <= DEC_CHUNK and tq % HG_CHUNK != 0

    lbs = jnp.cumsum(jax.nn.softmax(lb_logits.astype(F32), axis=0), axis=0)
    lb = lbs[l].reshape(1, WIDTH)
    n1 = norm1_w[l].reshape(1, D_MODEL)
    n2 = norm2_w[l].reshape(1, D_MODEL)
    w_in_b = w_in[l].astype(BF16)
    qw = jnp.tile(q_norm_w[l], N_HEADS).reshape(1, WIDTH)
    kw = jnp.tile(k_norm_w[l], N_HEADS).reshape(1, WIDTH)
    hgw = jnp.tile(hg_norm_w[l], N_HEADS).reshape(1, WIDTH)
    hd = np.arange(WIDTH) // HEAD_DIM
    bdiag = jnp.asarray((hd[:, None] == hd[None, :]).astype(np.float32), BF16)
    bias2 = sb_bias[l].astype(F32) * LOG2E
    wa = w_branch_a[l].astype(BF16)
    wb = w_branch_b[l].astype(BF16)
    wo = w_out[l].astype(BF16)
    wr32 = jnp.zeros((ROUTER_ROWS, D_MODEL), F32)
    wr32 = wr32.at[0:N_GROUPS].set(w_group_router[l].T).at[16:16 + N_EXPERTS].set(w_expert_router[l].T)
    wr = wr32.astype(BF16)
    br = jnp.full((ROUTER_ROWS, 1), NEG, F32)
    br = br.at[0:N_GROUPS, 0].set(b_group_router[l]).at[16:16 + N_EXPERTS, 0].set(b_expert_router[l])
    wg = w_gate_exp[l]
    wu = w_up_exp[l]
    wd = w_down_exp[l]

    def tri2_ge(n):
        r = np.arange(n)
        tri = (r[:, None] >= r[None, :]).astype(np.float32)
        return jnp.asarray(np.concatenate([tri, tri], axis=0), BF16)

    xp = x_prompt.reshape(n_p, D_MODEL)
    q, k, kb, v, vb, hq, hf, hi, hg, ga, gb = _in_proj(xp, n1, w_in_b, bdiag, qw, kw, t_len=t)
    o_sb = _sb_prompt(q.reshape(b, t, WIDTH), kb.reshape(b, t, WIDTH), vb.reshape(b, t, WIDTH), bias2,
                      tri2_ge(min(SB_TILE, t)))
    r3 = lambda a: a.reshape(b, t, WIDTH)
    y_b, st_p = _hgrn_prompt(r3(hq), r3(hf), r3(hi), r3(hg), lb, hgw,
                             _chunk_masks(min(HG_ROWS, t), HG_CHUNK, HG_CHUNK), bdiag)
    x1, eid, ew = _mix_out(xp, o_sb.reshape(n_p, WIDTH), y_b.reshape(n_p, WIDTH), ga, gb, wa, wb, wo, n2, wr, br)
    y_prompt = _moe(x1, n2, eid, ew, wg, wu, wd).reshape(b, t, D_MODEL)
    to_kv = lambda a: jnp.transpose(a.reshape(1, b, N_HEADS, HEAD_DIM, t), (0, 1, 4, 2, 3))
    new_k_prompt = to_kv(k)
    new_v_prompt = to_kv(v)
    new_hgrn_prompt = _pairs_to_state(st_p)[None]

    xs = x_sample.reshape(n_s, D_MODEL)
    q, k, v, hq, hf, hi, hg, ga, gb = _in_proj_precise(xs, n1, w_in[l], bdiag, qw, kw)
    pad_keys = lambda a: jnp.pad(jnp.swapaxes(a.reshape(bd_, tq, WIDTH), 1, 2), ((0, 0), (0, 0), (0, PAGE - tq)))
    n_pool = cache_k.shape[1]
    pages_t = lambda c: jnp.transpose(c, (0, 2, 3, 1)).reshape(n_pool, WIDTH, PAGE)
    o_sb = _sb_decode(q.reshape(bd_, tq, WIDTH), pad_keys(k), pad_keys(v), pages_t(cache_k[l]), pages_t(cache_v[l]),
                      page_table, bias2, tri2_ge(PAGE))
    pad_rows = lambda a: jnp.pad(a.reshape(bd_, tq, WIDTH), ((0, 0), (0, DEC_CHUNK - tq), (0, 0))).reshape(
        bd_ * DEC_CHUNK, WIDTH)
    y_b, st_s = _hgrn_decode(pad_rows(hq), pad_rows(hf), pad_rows(hi), pad_rows(hg),
                             _state_to_pairs(state_hgrn[l].astype(F32)), lb, hgw,
                             _chunk_masks(bd_ * DEC_CHUNK, DEC_CHUNK, tq), bdiag)
    y_b = y_b.reshape(bd_, DEC_CHUNK, WIDTH)[:, :tq].reshape(n_s, WIDTH)
    x1, eid, ew = _mix_out(xs, o_sb.reshape(n_s, WIDTH), y_b, ga, gb, w_branch_a[l], w_branch_b[l], w_out[l], n2,
                           wr32, br, precise=True)
    y_sample = _moe(x1, n2, eid, ew, wg, wu, wd).reshape(bd_, tq, D_MODEL)
    new_k_sample = k.reshape(1, bd_, tq, N_HEADS, HEAD_DIM)
    new_v_sample = v.reshape(1, bd_, tq, N_HEADS, HEAD_DIM)
    new_hgrn_sample = _pairs_to_state(st_s)[None]

    return (y_prompt, y_sample, new_k_prompt, new_v_prompt, new_hgrn_prompt,
            new_k_sample, new_v_sample, new_hgrn_sample)
```
